```python
import math
import jax
import jax.numpy as jnp
from jax import lax
import numpy as np

D_MODEL = 1024
BATCH = 16
SEQ = 2048
DEPTH = 2

GRID_W = 64
CTX_LEN = 256
HEAD_DIM = 64
ROPE_THETA = 10000.0
NA_HEADS = 8
NA_WIN_H = 8
NA_WIN_W = 16
NA_COL_BLOCK = 16
NA_COL_BAND = NA_WIN_W + NA_COL_BLOCK
DIFF_HEADS = 4
DIFF_VDIM = 2 * HEAD_DIM
GQA_Q_HEADS = 16
GQA_KV_HEADS = 4
GQA_GROUP = GQA_Q_HEADS // GQA_KV_HEADS
Q_BLOCK = 128
FFN_DIM = 2816
N_EXPERTS = 8
TOP_K = 2
EXPERT_DIM = 3584
MOE_BLOCK = 256
DEEPNORM_ALPHA = (2 * DEPTH) ** 0.25
DEEPNORM_BETA = (8 * DEPTH) ** -0.25
LN_EPS = 1e-5
RMS_EPS = 1e-6
NEG_INF = -1e30
NA_W = NA_HEADS * HEAD_DIM
DIFF_QK_W = DIFF_HEADS * 2 * HEAD_DIM
DIFF_V_W = DIFF_HEADS * DIFF_VDIM
EVEN_IN = 3 * NA_W + 2 * DIFF_QK_W + DIFF_V_W
MIX_W_EVEN = NA_W + DIFF_V_W
GQA_Q_W = GQA_Q_HEADS * HEAD_DIM
GQA_KV_W = GQA_KV_HEADS * HEAD_DIM
ODD_IN = GQA_Q_W + 2 * GQA_KV_W

kernel_name = 'hybrid_natten_diffattn_gqa_moe_dit'


def layer_norm(x, g, b):
    xf = x.astype(jnp.float32)
    mu = jnp.mean(xf, axis=-1, keepdims=True)
    var = jnp.mean(jnp.square(xf - mu), axis=-1, keepdims=True)
    return ((xf - mu) * lax.rsqrt(var + LN_EPS)).astype(x.dtype) * g + b


def rms_norm(x, g):
    xf = x.astype(jnp.float32)
    return (xf * lax.rsqrt(jnp.mean(xf * xf, axis=-1, keepdims=True) + RMS_EPS)).astype(x.dtype) * g


def ada_mods(cond, w, b):
    return jnp.split(jax.nn.silu(cond) @ w + b, 6, axis=-1)


def modulate(h, shift, scale):
    return h * (1 + scale) + shift


def swiglu(h, w13, w2):
    a, g = jnp.split(h @ w13, 2, axis=-1)
    return (jax.nn.silu(a) * g) @ w2


def axial_rope_tables(n_tok, dtype):
    t = jnp.arange(n_tok, dtype=jnp.int32)
    n_freq = HEAD_DIM // 4
    inv_freq = ROPE_THETA ** (-jnp.arange(n_freq, dtype=jnp.float32) / n_freq)
    ang = jnp.concatenate([(t // GRID_W).astype(jnp.float32)[:, None] * inv_freq,
                           (t % GRID_W).astype(jnp.float32)[:, None] * inv_freq], axis=-1)
    return jnp.cos(ang).astype(dtype), jnp.sin(ang).astype(dtype)


def apply_rope(x, cos, sin):
    shape = (cos.shape[0],) + (1,) * (x.ndim - 3) + (cos.shape[1],)
    c = cos.reshape(shape)
    s = sin.reshape(shape)
    x1, x2 = jnp.split(x, 2, axis=-1)
    return jnp.concatenate([x1 * c - x2 * s, x2 * c + x1 * s], axis=-1)


def sweep_query_blocks(fn, q, k, v):
    bsz, n_tok = q.shape[:2]
    qb = jnp.moveaxis(q.reshape((bsz, n_tok // Q_BLOCK, Q_BLOCK) + q.shape[2:]), 1, 0)
    out = lax.map(lambda qi: fn(qi, k, v), qb)
    return jnp.moveaxis(out, 0, 1).reshape((bsz, n_tok) + out.shape[3:])


def softmax_attention(q, k, v):
    s = jnp.einsum('bqhd,bkhd->bhqk', q, k, preferred_element_type=jnp.float32) * (q.shape[-1] ** -0.5)
    p = jax.nn.softmax(s, axis=-1).astype(v.dtype)
    return jnp.einsum('bhqk,bkhd->bqhd', p, v)


def diff_attend(q, k, v, lam):
    s = jnp.einsum('bqhcd,bkhcd->bhcqk', q, k, preferred_element_type=jnp.float32) * (q.shape[-1] ** -0.5)
    p = jax.nn.softmax(s, axis=-1)
    a = (p[:, :, 0] - lam * p[:, :, 1]).astype(v.dtype)
    return jnp.einsum('bhqk,bkhe->bqhe', a, v)


def gqa_attend(q, k, v):
    s = jnp.einsum('bqgrd,bkgd->bgrqk', q, k, preferred_element_type=jnp.float32) * (q.shape[-1] ** -0.5)
    p = jax.nn.softmax(s, axis=-1).astype(v.dtype)
    return jnp.einsum('bgrqk,bkgd->bqgrd', p, v)


def neighbourhood_attention(q, k, v, kc, vc, rpb):
    bsz, n_tok, n_h, dh = q.shape
    rows = n_tok // GRID_W
    kh = min(NA_WIN_H, rows)
    n_cb = GRID_W // NA_COL_BLOCK
    row_start = np.clip(np.arange(rows) - kh // 2, 0, rows - kh)
    dr_idx = row_start[:, None] + np.arange(kh)[None, :] - np.arange(rows)[:, None] + NA_WIN_H - 1
    q_cols = np.arange(GRID_W).reshape(n_cb, NA_COL_BLOCK)
    band_start = np.clip(q_cols[:, 0] - NA_WIN_W // 2, 0, GRID_W - NA_COL_BAND)
    band_cols = band_start[:, None] + np.arange(NA_COL_BAND)[None, :]
    win_start = np.clip(q_cols - NA_WIN_W // 2, 0, GRID_W - NA_WIN_W)
    bc = band_cols[:, None, :]
    in_win = (bc >= win_start[..., None]) & (bc < win_start[..., None] + NA_WIN_W)
    dc_idx = np.clip(bc - q_cols[..., None] + NA_WIN_W - 1, 0, 2 * NA_WIN_W - 2)
    bias = rpb.astype(jnp.float32)[:, dr_idx][..., dc_idx]
    bias = jnp.where(in_win, bias, NEG_INF).transpose(1, 0, 3, 4, 2, 5)
    scale = dh ** -0.5
    n_loc = kh * NA_COL_BAND
    qg = jnp.moveaxis(q.reshape(bsz, rows, n_cb, NA_COL_BLOCK, n_h, dh), 1, 0)
    kg = k.reshape(bsz, rows, GRID_W, n_h, dh)
    vg = v.reshape(bsz, rows, GRID_W, n_h, dh)

    def one_row(args):
        q_r, r0, bias_r = args
        k_band = lax.dynamic_slice_in_dim(kg, r0, kh, axis=1)[:, :, band_cols]
        v_band = lax.dynamic_slice_in_dim(vg, r0, kh, axis=1)[:, :, band_cols]
        s_loc = jnp.einsum('bnqhd,binjhd->bhnqij', q_r, k_band, preferred_element_type=jnp.float32) * scale + bias_r
        s_ctx = jnp.einsum('bnqhd,blhd->bhnql', q_r, kc, preferred_element_type=jnp.float32) * scale
        s = jnp.concatenate([s_loc.reshape(s_loc.shape[:4] + (n_loc,)), s_ctx], axis=-1)
        p = jax.nn.softmax(s, axis=-1).astype(v.dtype)
        p_loc = p[..., :n_loc].reshape(s_loc.shape)
        p_ctx = p[..., n_loc:]
        return (jnp.einsum('bhnqij,binjhd->bnqhd', p_loc, v_band)
                + jnp.einsum('bhnql,blhd->bnqhd', p_ctx, vc))

    out = lax.map(one_row, (qg, jnp.asarray(row_start, dtype=jnp.int32), bias))
    return jnp.moveaxis(out, 0, 1).reshape(bsz, n_tok, n_h * dh)


def moe_swiglu(h, router_w, w13, w2):
    hf = h.reshape(-1, h.shape[-1])
    n_tok = hf.shape[0]
    logits = jnp.matmul(hf, router_w, preferred_element_type=jnp.float32)
    top_v, top_i = lax.top_k(logits, TOP_K)
    gates = jax.nn.softmax(top_v, axis=-1)
    n_assign = n_tok * TOP_K
    flat_e = top_i.reshape(-1)
    flat_tok = jnp.repeat(jnp.arange(n_tok, dtype=jnp.int32), TOP_K)
    flat_g = gates.reshape(-1)
    order = jnp.argsort(flat_e)
    e_sorted = flat_e[order]
    counts = jnp.bincount(flat_e, length=N_EXPERTS)
    padded = (counts + MOE_BLOCK - 1) // MOE_BLOCK * MOE_BLOCK
    pad_end = jnp.cumsum(padded)
    pad_start = pad_end - padded
    start = jnp.cumsum(counts) - counts
    dest = pad_start[e_sorted] + jnp.arange(n_assign, dtype=jnp.int32) - start[e_sorted]
    n_blocks = -(-(n_assign + N_EXPERTS * (MOE_BLOCK - 1)) // MOE_BLOCK)
    n_slots = n_blocks * MOE_BLOCK
    slot_tok = jnp.zeros((n_slots,), jnp.int32).at[dest].set(flat_tok[order])
    slot_g = jnp.zeros((n_slots,), jnp.float32).at[dest].set(flat_g[order])
    block_e = jnp.minimum(jnp.searchsorted(pad_end, jnp.arange(n_blocks, dtype=jnp.int32) * MOE_BLOCK, side='right'),
                          N_EXPERTS - 1)

    def expert_block(args):
        tok, g, e = args
        return swiglu(hf[tok], w13[e], w2[e]) * g[:, None].astype(hf.dtype)

    y = lax.map(expert_block, (slot_tok.reshape(n_blocks, MOE_BLOCK), slot_g.reshape(n_blocks, MOE_BLOCK), block_e))
    out = jnp.zeros_like(hf).at[slot_tok].add(y.reshape(n_slots, hf.shape[-1]))
    return out.reshape(h.shape)


def split_even(p):
    lead = p.shape[:2]
    cuts = np.cumsum([NA_W, NA_W, NA_W, DIFF_QK_W, DIFF_QK_W]).tolist()
    qa, ka, va, qb, kb, vb = jnp.split(p, cuts, axis=-1)
    return (qa.reshape(lead + (NA_HEADS, HEAD_DIM)), ka.reshape(lead + (NA_HEADS, HEAD_DIM)),
            va.reshape(lead + (NA_HEADS, HEAD_DIM)), qb.reshape(lead + (DIFF_HEADS, 2, HEAD_DIM)),
            kb.reshape(lead + (DIFF_HEADS, 2, HEAD_DIM)), vb.reshape(lead + (DIFF_HEADS, DIFF_VDIM)))


def split_odd(p):
    lead = p.shape[:2]
    q, k, v = jnp.split(p, [GQA_Q_W, GQA_Q_W + GQA_KV_W], axis=-1)
    return (q.reshape(lead + (GQA_Q_HEADS, HEAD_DIM)), k.reshape(lead + (GQA_KV_HEADS, HEAD_DIM)),
            v.reshape(lead + (GQA_KV_HEADS, HEAD_DIM)))


def even_layer(x, ctx, c, c_ctx, params, layer_idx, last):
    (ada_w, ada_b, w_in, rpb, lambda_qk, subln_g, w_out,
     ln1_g, ln1_b, ffn_w13, ffn_w2, ln2_g, ln2_b) = params
    bsz, n_tok, _ = x.shape
    n_ctx = ctx.shape[1]
    sh1, sc1, g1, sh2, sc2, g2 = [m[:, None, :] for m in ada_mods(c, ada_w, ada_b)]
    csh1, csc1, cg1, csh2, csc2, cg2 = ada_mods(c_ctx, ada_w, ada_b)
    qa, ka, va, qb, kb, vb = split_even(modulate(x, sh1, sc1) @ w_in)
    qa_c, ka_c, va_c, qb_c, kb_c, vb_c = split_even(modulate(ctx, csh1, csc1) @ w_in)
    cos, sin = axial_rope_tables(n_tok, x.dtype)
    y_a = neighbourhood_attention(qa, ka, va, ka_c, va_c, rpb)
    lam_init = 0.8 - 0.6 * math.exp(-0.3 * layer_idx)
    lq = lambda_qk.astype(jnp.float32)
    lam = jnp.exp(jnp.sum(lq[0] * lq[1])) - jnp.exp(jnp.sum(lq[2] * lq[3])) + lam_init
    kb_all = jnp.concatenate([apply_rope(kb, cos, sin), kb_c], axis=1)
    vb_all = jnp.concatenate([vb, vb_c], axis=1)
    y_b = sweep_query_blocks(lambda qi, kk, vv: diff_attend(qi, kk, vv, lam), apply_rope(qb, cos, sin), kb_all, vb_all)
    y_b = rms_norm(y_b, subln_g) * (1.0 - lam_init)
    y = jnp.concatenate([y_a, y_b.reshape(bsz, n_tok, DIFF_V_W)], axis=-1) @ w_out
    x_new = layer_norm(DEEPNORM_ALPHA * x + g1 * y, ln1_g, ln1_b)
    x_new = layer_norm(DEEPNORM_ALPHA * x_new + g2 * swiglu(modulate(x_new, sh2, sc2), ffn_w13, ffn_w2), ln2_g, ln2_b)
    if not last:
        ya_c = softmax_attention(qa_c, ka_c, va_c).reshape(bsz, n_ctx, NA_W)
        yb_c = rms_norm(diff_attend(qb_c, kb_c, vb_c, lam), subln_g) * (1.0 - lam_init)
        yc = jnp.concatenate([ya_c, yb_c.reshape(bsz, n_ctx, DIFF_V_W)], axis=-1) @ w_out
        ctx = layer_norm(DEEPNORM_ALPHA * ctx + cg1 * yc, ln1_g, ln1_b)
        ctx = layer_norm(DEEPNORM_ALPHA * ctx + cg2 * swiglu(modulate(ctx, csh2, csc2), ffn_w13, ffn_w2), ln2_g, ln2_b)
    return x_new, ctx


def odd_layer(x, ctx, c, c_ctx, params, layer_idx, last):
    (ada_w, ada_b, w_in, q_norm_g, k_norm_g, w_out, ln1_g, ln1_b,
     router_w, moe_w13, moe_w2, ln2_g, ln2_b) = params
    bsz, n_tok, _ = x.shape
    n_ctx = ctx.shape[1]
    sh1, sc1, g1, sh2, sc2, g2 = [m[:, None, :] for m in ada_mods(c, ada_w, ada_b)]
    csh1, csc1, cg1, csh2, csc2, cg2 = ada_mods(c_ctx, ada_w, ada_b)
    q, k, v = split_odd(modulate(x, sh1, sc1) @ w_in)
    q_c, k_c, v_c = split_odd(modulate(ctx, csh1, csc1) @ w_in)
    cos, sin = axial_rope_tables(n_tok, x.dtype)
    q = apply_rope(rms_norm(q, q_norm_g), cos, sin)
    k = apply_rope(rms_norm(k, k_norm_g), cos, sin)
    k_c = rms_norm(k_c, k_norm_g)
    k_all = jnp.concatenate([k, k_c], axis=1)
    v_all = jnp.concatenate([v, v_c], axis=1)
    qg = q.reshape(bsz, n_tok, GQA_KV_HEADS, GQA_GROUP, HEAD_DIM)
    y = sweep_query_blocks(gqa_attend, qg, k_all, v_all).reshape(bsz, n_tok, GQA_Q_W) @ w_out
    x_new = layer_norm(DEEPNORM_ALPHA * x + g1 * y, ln1_g, ln1_b)
    x_new = layer_norm(DEEPNORM_ALPHA * x_new + g2 * moe_swiglu(modulate(x_new, sh2, sc2), router_w, moe_w13, moe_w2),
                       ln2_g, ln2_b)
    if not last:
        qc = rms_norm(q_c, q_norm_g).reshape(bsz, n_ctx, GQA_KV_HEADS, GQA_GROUP, HEAD_DIM)
        yc = gqa_attend(qc, k_c, v_c).reshape(bsz, n_ctx, GQA_Q_W) @ w_out
        ctx = layer_norm(DEEPNORM_ALPHA * ctx + cg1 * yc, ln1_g, ln1_b)
        ctx = layer_norm(DEEPNORM_ALPHA * ctx + cg2 * moe_swiglu(modulate(ctx, csh2, csc2), router_w, moe_w13, moe_w2),
                         ln2_g, ln2_b)
    return x_new, ctx


def setup_inputs(seed: int = 0) -> dict:
    key = jax.random.key(seed)
    keys = iter(jax.random.split(key, 64))

    def nrm(shape, std):
        return jax.random.normal(next(keys), shape, jnp.float32) * std

    def gain(n):
        return 1.0 + nrm((n,), 0.05)

    D = D_MODEL
    s_in = D ** -0.5
    return {
        'x': nrm((BATCH, SEQ, D), 1.0),
        'c': nrm((BATCH, D), 1.0),
        'ctx': nrm((BATCH, CTX_LEN, D), 1.0),
        'c_ctx': nrm((D,), 1.0),
        'l0_ada_w': nrm((D, 6 * D), 0.5 * s_in),
        'l0_ada_b': nrm((6 * D,), 0.02),
        'l0_w_in': nrm((D, EVEN_IN), s_in),
        'l0_rpb': nrm((NA_HEADS, 2 * NA_WIN_H - 1, 2 * NA_WIN_W - 1), 0.1),
        'l0_lambda_qk': nrm((4, HEAD_DIM), 0.1),
        'l0_subln_g': gain(DIFF_VDIM),
        'l0_w_out': nrm((MIX_W_EVEN, D), MIX_W_EVEN ** -0.5 * DEEPNORM_BETA),
        'l0_ln1_g': gain(D),
        'l0_ln1_b': nrm((D,), 0.02),
        'l0_ffn_w13': nrm((D, 2 * FFN_DIM), s_in),
        'l0_ffn_w2': nrm((FFN_DIM, D), FFN_DIM ** -0.5 * DEEPNORM_BETA),
        'l0_ln2_g': gain(D),
        'l0_ln2_b': nrm((D,), 0.02),
        'l1_ada_w': nrm((D, 6 * D), 0.5 * s_in),
        'l1_ada_b': nrm((6 * D,), 0.02),
        'l1_w_in': nrm((D, ODD_IN), s_in),
        'l1_q_norm_g': gain(HEAD_DIM),
        'l1_k_norm_g': gain(HEAD_DIM),
        'l1_w_out': nrm((GQA_Q_W, D), GQA_Q_W ** -0.5 * DEEPNORM_BETA),
        'l1_ln1_g': gain(D),
        'l1_ln1_b': nrm((D,), 0.02),
        'l1_router_w': nrm((D, N_EXPERTS), s_in),
        'l1_moe_w13': nrm((N_EXPERTS, D, 2 * EXPERT_DIM), s_in),
        'l1_moe_w2': nrm((N_EXPERTS, EXPERT_DIM, D), EXPERT_DIM ** -0.5 * DEEPNORM_BETA),
        'l1_ln2_g': gain(D),
        'l1_ln2_b': nrm((D,), 0.02),
    }


def reference(x, c, ctx, c_ctx,
              l0_ada_w, l0_ada_b, l0_w_in, l0_rpb, l0_lambda_qk, l0_subln_g, l0_w_out,
              l0_ln1_g, l0_ln1_b, l0_ffn_w13, l0_ffn_w2, l0_ln2_g, l0_ln2_b,
              l1_ada_w, l1_ada_b, l1_w_in, l1_q_norm_g, l1_k_norm_g, l1_w_out,
              l1_ln1_g, l1_ln1_b, l1_router_w, l1_moe_w13, l1_moe_w2, l1_ln2_g, l1_ln2_b):
    layer_params = (
        (l0_ada_w, l0_ada_b, l0_w_in, l0_rpb, l0_lambda_qk, l0_subln_g, l0_w_out,
         l0_ln1_g, l0_ln1_b, l0_ffn_w13, l0_ffn_w2, l0_ln2_g, l0_ln2_b),
        (l1_ada_w, l1_ada_b, l1_w_in, l1_q_norm_g, l1_k_norm_g, l1_w_out,
         l1_ln1_g, l1_ln1_b, l1_router_w, l1_moe_w13, l1_moe_w2, l1_ln2_g, l1_ln2_b),
    )
    for layer in range(DEPTH):
        last = layer == DEPTH - 1
        if layer % 2 == 0:
            x, ctx = even_layer(x, ctx, c, c_ctx, layer_params[layer], layer, last)
        else:
            x, ctx = odd_layer(x, ctx, c, c_ctx, layer_params[layer], layer, last)
    return x
```

```python
import functools
import math

import jax
import jax.numpy as jnp
import numpy as np
from jax import lax
from jax.experimental import pallas as pl
from jax.experimental.pallas import tpu as pltpu

F32 = jnp.float32
BF16 = jnp.bfloat16

D_MODEL = 1024
DEPTH = 2
GRID_W = 64
HEAD_DIM = 64
ROPE_THETA = 10000.0
NA_HEADS = 8
NA_WIN_H = 8
NA_WIN_W = 16
DIFF_HEADS = 4
DIFF_VDIM = 2 * HEAD_DIM
GQA_Q_HEADS = 16
GQA_KV_HEADS = 4
GQA_GROUP = GQA_Q_HEADS // GQA_KV_HEADS
FFN_DIM = 2816
N_EXPERTS = 8
TOP_K = 2
EXPERT_DIM = 3584
DEEPNORM_ALPHA = (2 * DEPTH) ** 0.25
LN_EPS = 1e-5
RMS_EPS = 1e-6
NEG_INF = -1e30
QK_SCALE = HEAD_DIM ** -0.5

LANES = 128
VMEM_LIMIT = 56 * 1024 * 1024

NA_TILE_ROWS = 4
NA_KEY_ROWS = 12
NA_TQ = NA_TILE_ROWS * GRID_W
NA_KEY_CHUNK = NA_TQ
NA_N_KEY_CHUNKS = NA_KEY_ROWS * GRID_W // NA_KEY_CHUNK

MOE_TM = 1024
MOE_TF = 512


def _cparams(sem, vmem=VMEM_LIMIT):
    return pltpu.CompilerParams(dimension_semantics=sem, vmem_limit_bytes=vmem)


def _dot(a, b):
    return jnp.dot(a, b, preferred_element_type=F32)


def _dot_t(a, b):
    return lax.dot_general(a, b, (((1,), (1,)), ((), ())), preferred_element_type=F32)


def _split(a):
    hi = a.astype(BF16)
    lo = (a - hi.astype(F32)).astype(BF16)
    return hi, lo


def _dot3(a, b):
    a_hi, a_lo = _split(a)
    b_hi, b_lo = _split(b)
    return _dot(a_hi, b_hi) + (_dot(a_hi, b_lo) + _dot(a_lo, b_hi))


def _dot2(a, b_exact):
    a_hi, a_lo = _split(a)
    return _dot(a_hi, b_exact) + _dot(a_lo, b_exact)


def _layer_norm(z, g, b):
    mu = jnp.mean(z, axis=-1, keepdims=True)
    zc = z - mu
    var = jnp.mean(zc * zc, axis=-1, keepdims=True)
    return zc * lax.rsqrt(var + LN_EPS) * g + b


def _silu(a):
    return a * jax.nn.sigmoid(a)


def _rope128(y, cos, sin_signed, lane_hi):
    swapped = jnp.where(lane_hi, pltpu.roll(y, 32, 1), pltpu.roll(y, 96, 1))
    return y * cos + swapped * sin_signed


def _store_groups(y, refs_iter):
    for g in range(y.shape[1] // HEAD_DIM):
        ref, idx = next(refs_iter)
        ref[0, idx] = y[:, g * HEAD_DIM:(g + 1) * HEAD_DIM].astype(BF16)


def _ada_kernel(c_ref, w_ref, b_ref, o_ref):
    o_ref[...] = _dot3(_silu(c_ref[...]), w_ref[...]) + b_ref[...]


def _ada_mods(cond, w, b):
    rows, tn = cond.shape[0], 1536
    n_out = w.shape[1]
    return pl.pallas_call(
        _ada_kernel,
        grid=(n_out // tn,),
        in_specs=[pl.BlockSpec((rows, D_MODEL), lambda j: (0, 0)),
                  pl.BlockSpec((D_MODEL, tn), lambda j: (0, j)),
                  pl.BlockSpec((1, tn), lambda j: (0, j))],
        out_specs=pl.BlockSpec((rows, tn), lambda j: (0, j)),
        out_shape=jax.ShapeDtypeStruct((rows, n_out), F32),
        compiler_params=_cparams(("arbitrary",)),
        name="ada_mods",
    )(cond, w, b.reshape(1, n_out))


def _inproj0_kernel(x_ref, sh_ref, sc_ref, w_ref, cos_ref, sin_ref,
                    qa_ref, ka_ref, va_ref, qb_ref, kb_ref, vb_ref, *, rope):
    h = (x_ref[0] * (1.0 + sc_ref[0]) + sh_ref[0]).astype(BF16)
    lane_hi = (lax.broadcasted_iota(jnp.int32, (1, LANES), 1) % HEAD_DIM) >= HEAD_DIM // 2
    sec = NA_HEADS * HEAD_DIM
    plan = ((qa_ref, QK_SCALE, False), (ka_ref, None, False), (va_ref, None, False),
            (qb_ref, QK_SCALE, True), (kb_ref, None, True))
    for s, (ref, scale, rotary) in enumerate(plan):
        p = _dot(h, w_ref[:, s * sec:(s + 1) * sec])
        if scale is not None:
            p = p * scale
        for c in range(sec // LANES):
            y = p[:, c * LANES:(c + 1) * LANES]
            if rotary and rope:
                y = _rope128(y, cos_ref[...], sin_ref[...], lane_hi)
            ref[0, 2 * c] = y[:, :HEAD_DIM].astype(BF16)
            ref[0, 2 * c + 1] = y[:, HEAD_DIM:].astype(BF16)
    p = _dot(h, w_ref[:, 5 * sec:6 * sec])
    for hh in range(DIFF_HEADS):
        vb_ref[0, hh] = p[:, hh * DIFF_VDIM:(hh + 1) * DIFF_VDIM].astype(BF16)


def _inproj0(x, shift, scale, w, cos, sin, *, tm, rope):
    bsz, n_tok, _ = x.shape
    per_batch = shift.shape[0] > 1
    mod_map = (lambda b, i: (b, 0, 0)) if per_batch else (lambda b, i: (0, 0, 0))
    hd = lambda n, d: jax.ShapeDtypeStruct((bsz, n, n_tok, d), BF16)
    hspec = lambda n, d: pl.BlockSpec((1, n, tm, d), lambda b, i: (b, 0, i, 0))
    return pl.pallas_call(
        functools.partial(_inproj0_kernel, rope=rope),
        grid=(bsz, n_tok // tm),
        in_specs=[pl.BlockSpec((1, tm, D_MODEL), lambda b, i: (b, i, 0)),
                  pl.BlockSpec((1, 1, D_MODEL), mod_map),
                  pl.BlockSpec((1, 1, D_MODEL), mod_map),
                  pl.BlockSpec(w.shape, lambda b, i: (0, 0)),
                  pl.BlockSpec((tm, LANES), lambda b, i: (i, 0)),
                  pl.BlockSpec((tm, LANES), lambda b, i: (i, 0))],
        out_specs=[hspec(8, 64), hspec(8, 64), hspec(8, 64), hspec(8, 64), hspec(8, 64), hspec(4, 128)],
        out_shape=[hd(8, 64), hd(8, 64), hd(8, 64), hd(8, 64), hd(8, 64), hd(4, 128)],
        compiler_params=_cparams(("arbitrary", "arbitrary")),
        name="inproj0",
    )(x, shift, scale, w, cos, sin)


def _scores(q, ks, biases):
    out = []
    for k, bias in zip(ks, biases):
        s = _dot_t(q, k)
        out.append(s if bias is None else s + bias)
    return out


def _softmax_parts(ss):
    m = ss[0].max(axis=-1, keepdims=True)
    for s in ss[1:]:
        m = jnp.maximum(m, s.max(axis=-1, keepdims=True))
    es = [jnp.exp(s - m) for s in ss]
    l = es[0].sum(axis=-1, keepdims=True)
    for e in es[1:]:
        l = l + e.sum(axis=-1, keepdims=True)
    return es, l


def _softmax_attend(q, ks, vs, biases):
    es, l = _softmax_parts(_scores(q, ks, biases))
    o = _dot(es[0].astype(BF16), vs[0])
    for e, v in zip(es[1:], vs[1:]):
        o = o + _dot(e.astype(BF16), v)
    return o * (1.0 / l)


def _diff_attend(q0, q1, k0s, k1s, vs, lam, subln_g, out_scale):
    e0, l0 = _softmax_parts(_scores(q0, k0s, [None] * len(k0s)))
    e1, l1 = _softmax_parts(_scores(q1, k1s, [None] * len(k1s)))
    r0 = 1.0 / l0
    r1 = lam / l1
    y = None
    for a0, a1, v in zip(e0, e1, vs):
        t = _dot((a0 * r0 - a1 * r1).astype(BF16), v)
        y = t if y is None else y + t
    y = y * lax.rsqrt(jnp.mean(y * y, axis=-1, keepdims=True) + RMS_EPS) * subln_g
    return y * out_scale


def _na_kernel(q_ref, k0_ref, k1_ref, k2_ref, v0_ref, v1_ref, v2_ref, kc_ref, vc_ref, bias_ref,
               o_ref, acc_ref):
    for h in range(NA_HEADS):
        ks = [k0_ref[0, h], k1_ref[0, h], k2_ref[0, h], kc_ref[0, h]]
        vs = [v0_ref[0, h], v1_ref[0, h], v2_ref[0, h], vc_ref[0, h]]
        biases = [bias_ref[0, h, :, j * NA_KEY_CHUNK:(j + 1) * NA_KEY_CHUNK] for j in range(NA_N_KEY_CHUNKS)]
        acc_ref[:, h * HEAD_DIM:(h + 1) * HEAD_DIM] = _softmax_attend(q_ref[0, h], ks, vs, biases + [None])
    o_ref[0] = acc_ref[...].astype(BF16)


def _na_bias_table(rpb):
    rows_total = 32
    tables_dr, tables_dc, tables_ok = [], [], []
    for r0 in (0, 4, 24, 28):
        ws = int(np.clip(r0 - NA_WIN_H // 2, 0, rows_total - NA_KEY_ROWS))
        q_r = r0 + np.arange(NA_TQ) // GRID_W
        q_c = np.arange(NA_TQ) % GRID_W
        k_r = ws + np.arange(NA_KEY_ROWS * GRID_W) // GRID_W
        k_c = np.arange(NA_KEY_ROWS * GRID_W) % GRID_W
        row_start = np.clip(q_r - NA_WIN_H // 2, 0, rows_total - NA_WIN_H)
        col_start = np.clip(q_c - NA_WIN_W // 2, 0, GRID_W - NA_WIN_W)
        ok = ((k_r[None, :] >= row_start[:, None]) & (k_r[None, :] < row_start[:, None] + NA_WIN_H)
              & (k_c[None, :] >= col_start[:, None]) & (k_c[None, :] < col_start[:, None] + NA_WIN_W))
        dr = np.clip(k_r[None, :] - q_r[:, None] + NA_WIN_H - 1, 0, 2 * NA_WIN_H - 2)
        dc = np.clip(k_c[None, :] - q_c[:, None] + NA_WIN_W - 1, 0, 2 * NA_WIN_W - 2)
        tables_dr.append(dr)
        tables_dc.append(dc)
        tables_ok.append(ok)
    dr = np.stack(tables_dr)
    dc = np.stack(tables_dc)
    ok = np.stack(tables_ok)
    flat = rpb.astype(F32).reshape(NA_HEADS, -1)
    gathered = jnp.take(flat, jnp.asarray(dr * (2 * NA_WIN_W - 1) + dc), axis=1)
    return jnp.where(jnp.asarray(ok)[None], gathered, NEG_INF).transpose(1, 0, 2, 3)


def _na_attention(qa, ka, va, ka_c, va_c, bias_tab):
    bsz, n_h, n_tok, _ = qa.shape
    n_ctx = ka_c.shape[2]
    n_tiles = n_tok // NA_TQ
    last_start = (n_tok - NA_KEY_ROWS * GRID_W) // NA_KEY_CHUNK

    def win(j):
        return lambda t, b: (b, 0, jnp.clip(t - 1, 0, last_start) + j, 0)

    kv_spec = lambda j: pl.BlockSpec((1, n_h, NA_KEY_CHUNK, HEAD_DIM), win(j))
    ctx_spec = pl.BlockSpec((1, n_h, n_ctx, HEAD_DIM), lambda t, b: (b, 0, 0, 0))
    bias_map = lambda t, b: (jnp.minimum(t, 1) + jnp.maximum(t - (n_tiles - 3), 0), 0, 0, 0)
    return pl.pallas_call(
        _na_kernel,
        grid=(n_tiles, bsz),
        in_specs=[pl.BlockSpec((1, n_h, NA_TQ, HEAD_DIM), lambda t, b: (b, 0, t, 0)),
                  kv_spec(0), kv_spec(1), kv_spec(2), kv_spec(0), kv_spec(1), kv_spec(2),
                  ctx_spec, ctx_spec,
                  pl.BlockSpec((1, n_h, NA_TQ, NA_KEY_ROWS * GRID_W), bias_map)],
        out_specs=pl.BlockSpec((1, NA_TQ, n_h * HEAD_DIM), lambda t, b: (b, t, 0)),
        out_shape=jax.ShapeDtypeStruct((bsz, n_tok, n_h * HEAD_DIM), BF16),
        scratch_shapes=[pltpu.VMEM((NA_TQ, n_h * HEAD_DIM), F32)],
        compiler_params=_cparams(("arbitrary", "arbitrary")),
        name="na_attention",
    )(qa, ka, ka, ka, va, va, va, ka_c, va_c, bias_tab)


def _diff_kernel(lam_ref, q_ref, k_ref, kc_ref, v_ref, vc_ref, g_ref, o_ref, *, out_scale, with_latent):
    lam = lam_ref[0, 0]
    for h in range(DIFF_HEADS):
        if with_latent:
            k0s = [k_ref[0, 2 * h], kc_ref[0, 2 * h]]
            k1s = [k_ref[0, 2 * h + 1], kc_ref[0, 2 * h + 1]]
            vs = [v_ref[0, h], vc_ref[0, h]]
        else:
            k0s, k1s, vs = [kc_ref[0, 2 * h]], [kc_ref[0, 2 * h + 1]], [vc_ref[0, h]]
        y = _diff_attend(q_ref[0, 2 * h], q_ref[0, 2 * h + 1], k0s, k1s, vs, lam, g_ref[...], out_scale)
        o_ref[0, :, h * DIFF_VDIM:(h + 1) * DIFF_VDIM] = y.astype(BF16)


def _diff_attention(lam, qb, kb, vb, kb_c, vb_c, subln_g, *, tq, out_scale):
    bsz, _, n_tok, _ = qb.shape
    n_ctx = kb_c.shape[2]
    full = lambda a: pl.BlockSpec((1,) + a.shape[1:], lambda b, i: (b, 0, 0, 0))
    return pl.pallas_call(
        functools.partial(_diff_kernel, out_scale=out_scale, with_latent=True),
        grid=(bsz, n_tok // tq),
        in_specs=[pl.BlockSpec(memory_space=pltpu.SMEM),
                  pl.BlockSpec((1, 2 * DIFF_HEADS, tq, HEAD_DIM), lambda b, i: (b, 0, i, 0)),
                  full(kb), full(kb_c), full(vb), full(vb_c),
                  pl.BlockSpec((1, DIFF_VDIM), lambda b, i: (0, 0))],
        out_specs=pl.BlockSpec((1, tq, DIFF_HEADS * DIFF_VDIM), lambda b, i: (b, i, 0)),
        out_shape=jax.ShapeDtypeStruct((bsz, n_tok, DIFF_HEADS * DIFF_VDIM), BF16),
        compiler_params=_cparams(("arbitrary", "arbitrary")),
        name="diff_attention",
    )(lam, qb, kb, kb_c, vb, vb_c, subln_g.reshape(1, DIFF_VDIM))


def _ctx_attn0_kernel(lam_ref, qa_ref, ka_ref, va_ref, qb_ref, kb_ref, vb_ref, g_ref, ya_ref, yb_ref, acc_ref,
                      *, out_scale):
    for h in range(NA_HEADS):
        acc_ref[:, h * HEAD_DIM:(h + 1) * HEAD_DIM] = _softmax_attend(
            qa_ref[0, h], [ka_ref[0, h]], [va_ref[0, h]], [None])
    ya_ref[0] = acc_ref[...].astype(BF16)
    _diff_kernel(lam_ref, qb_ref, None, kb_ref, None, vb_ref, g_ref, yb_ref,
                 out_scale=out_scale, with_latent=False)


def _ctx_attention0(lam, qa, ka, va, qb, kb, vb, subln_g, *, out_scale):
    bsz, _, n_ctx, _ = qa.shape
    full = lambda a: pl.BlockSpec((1,) + a.shape[1:], lambda b: (b, 0, 0, 0))
    width = NA_HEADS * HEAD_DIM
    out = jax.ShapeDtypeStruct((bsz, n_ctx, width), BF16)
    return pl.pallas_call(
        functools.partial(_ctx_attn0_kernel, out_scale=out_scale),
        grid=(bsz,),
        in_specs=[pl.BlockSpec(memory_space=pltpu.SMEM),
                  full(qa), full(ka), full(va), full(qb), full(kb), full(vb),
                  pl.BlockSpec((1, DIFF_VDIM), lambda b: (0, 0))],
        out_specs=[pl.BlockSpec((1, n_ctx, width), lambda b: (b, 0, 0))] * 2,
        out_shape=[out, out],
        scratch_shapes=[pltpu.VMEM((n_ctx, width), F32)],
        compiler_params=_cparams(("arbitrary",)),
        name="ctx_attention0",
    )(lam, qa, ka, va, qb, kb, vb, subln_g.reshape(1, DIFF_VDIM))


FFN_CHUNK = FFN_DIM // 2


def _post0_kernel(x_ref, ya_ref, yb_ref, g1_ref, sh2_ref, sc2_ref, g2_ref, wo_ref, w13_ref, w2_ref,
                  ln1g_ref, ln1b_ref, ln2g_ref, ln2b_ref, o_ref):
    half = wo_ref.shape[0] // 2
    y = _dot(ya_ref[0], wo_ref[:half, :]) + _dot(yb_ref[0], wo_ref[half:, :])
    x1 = _layer_norm(DEEPNORM_ALPHA * x_ref[0] + g1_ref[0] * y, ln1g_ref[...], ln1b_ref[...])
    h = (x1 * (1.0 + sc2_ref[0]) + sh2_ref[0]).astype(BF16)
    f = None
    for c in range(FFN_DIM // FFN_CHUNK):
        a = _dot(h, w13_ref[:, c * FFN_CHUNK:(c + 1) * FFN_CHUNK])
        g = _dot(h, w13_ref[:, FFN_DIM + c * FFN_CHUNK:FFN_DIM + (c + 1) * FFN_CHUNK])
        t = _dot((_silu(a) * g).astype(BF16), w2_ref[c * FFN_CHUNK:(c + 1) * FFN_CHUNK, :])
        f = t if f is None else f + t
    o_ref[0] = _layer_norm(DEEPNORM_ALPHA * x1 + g2_ref[0] * f, ln2g_ref[...], ln2b_ref[...])


def _post0(x, ya, yb, g1, sh2, sc2, g2, wo, w13, w2, ln1g, ln1b, ln2g, ln2b, *, tm):
    bsz, n_tok, _ = x.shape
    per_batch = g1.shape[0] > 1
    mod_map = (lambda b, i: (b, 0, 0)) if per_batch else (lambda b, i: (0, 0, 0))
    mod = pl.BlockSpec((1, 1, D_MODEL), mod_map)
    const = lambda a: pl.BlockSpec(a.shape, lambda b, i: (0,) * a.ndim, pipeline_mode=pl.Buffered(1))
    vec = lambda a: a.reshape(1, D_MODEL)
    half = ya.shape[2]
    return pl.pallas_call(
        _post0_kernel,
        grid=(bsz, n_tok // tm),
        in_specs=[pl.BlockSpec((1, tm, D_MODEL), lambda b, i: (b, i, 0)),
                  pl.BlockSpec((1, tm, half), lambda b, i: (b, i, 0)),
                  pl.BlockSpec((1, tm, half), lambda b, i: (b, i, 0)),
                  mod, mod, mod, mod, const(wo), const(w13), const(w2),
                  const(vec(ln1g)), const(vec(ln1b)), const(vec(ln2g)), const(vec(ln2b))],
        out_specs=pl.BlockSpec((1, tm, D_MODEL), lambda b, i: (b, i, 0)),
        out_shape=jax.ShapeDtypeStruct(x.shape, F32),
        compiler_params=_cparams(("arbitrary", "arbitrary")),
        name="post0",
    )(x, ya, yb, g1, sh2, sc2, g2, wo, w13, w2, vec(ln1g), vec(ln1b), vec(ln2g), vec(ln2b))


NORM_CHUNK = 4 * HEAD_DIM


def _inproj1_kernel(x_ref, sh_ref, sc_ref, w_ref, gain_ref, ind_ref, indt_ref, cos_ref, sin_ref, *out_refs,
                    n_norm_chunks, group_counts, rope):
    h = (x_ref[0] * (1.0 + sc_ref[0]) + sh_ref[0]).astype(BF16)
    lane_hi = (lax.broadcasted_iota(jnp.int32, (1, LANES), 1) % HEAD_DIM) >= HEAD_DIM // 2
    targets = iter([(ref, i) for ref, n in zip(out_refs, group_counts) for i in range(n)])
    n_chunks = w_ref.shape[1] // NORM_CHUNK
    for c in range(n_chunks):
        cols = slice(c * NORM_CHUNK, (c + 1) * NORM_CHUNK)
        p = _dot(h, w_ref[:, cols])
        if c < n_norm_chunks:
            ms = _dot2(p * p, ind_ref[...]) * (1.0 / HEAD_DIM)
            r = _dot2(lax.rsqrt(ms + RMS_EPS), indt_ref[...])
            p = p * r * gain_ref[:, cols]
            if rope:
                p = jnp.concatenate(
                    [_rope128(p[:, j * LANES:(j + 1) * LANES], cos_ref[...], sin_ref[...], lane_hi)
                     for j in range(NORM_CHUNK // LANES)], axis=1)
        _store_groups(p, targets)


def _inproj1(x, shift, scale, w, gain, cos, sin, *, tm, group_counts, n_norm_chunks, rope):
    bsz, n_tok, _ = x.shape
    per_batch = shift.shape[0] > 1
    mod_map = (lambda b, i: (b, 0, 0)) if per_batch else (lambda b, i: (0, 0, 0))
    ind_np = np.zeros((NORM_CHUNK, LANES), np.float32)
    ind_np[np.arange(NORM_CHUNK), np.arange(NORM_CHUNK) // HEAD_DIM] = 1.0
    ind = jnp.asarray(ind_np, BF16)
    indt = jnp.asarray(ind_np.T, BF16)
    const = lambda a: pl.BlockSpec(a.shape, lambda b, i: (0,) * a.ndim)
    return pl.pallas_call(
        functools.partial(_inproj1_kernel, n_norm_chunks=n_norm_chunks, group_counts=group_counts, rope=rope),
        grid=(bsz, n_tok // tm),
        in_specs=[pl.BlockSpec((1, tm, D_MODEL), lambda b, i: (b, i, 0)),
                  pl.BlockSpec((1, 1, D_MODEL), mod_map),
                  pl.BlockSpec((1, 1, D_MODEL), mod_map),
                  const(w), const(gain), const(ind), const(indt),
                  pl.BlockSpec((tm, LANES), lambda b, i: (i, 0)),
                  pl.BlockSpec((tm, LANES), lambda b, i: (i, 0))],
        out_specs=[pl.BlockSpec((1, n, tm, HEAD_DIM), lambda b, i: (b, 0, i, 0)) for n in group_counts],
        out_shape=[jax.ShapeDtypeStruct((bsz, n, n_tok, HEAD_DIM), BF16) for n in group_counts],
        compiler_params=_cparams(("arbitrary", "arbitrary")),
        name="inproj1",
    )(x, shift, scale, w, gain, ind, indt, cos, sin)


def _gqa_kernel(q_ref, k_ref, kc_ref, v_ref, vc_ref, o_ref, acc_ref):
    for h in range(GQA_Q_HEADS):
        g = h // GQA_GROUP
        acc_ref[:, h * HEAD_DIM:(h + 1) * HEAD_DIM] = _softmax_attend(
            q_ref[0, h], [k_ref[0, g], kc_ref[0, g]], [v_ref[0, g], vc_ref[0, g]], [None, None])
    o_ref[0] = acc_ref[...].astype(BF16)


def _gqa_attention(q, k, v, k_c, v_c, *, tq):
    bsz, n_h, n_tok, _ = q.shape
    full = lambda a: pl.BlockSpec((1,) + a.shape[1:], lambda b, i: (b, 0, 0, 0))
    return pl.pallas_call(
        _gqa_kernel,
        grid=(bsz, n_tok // tq),
        in_specs=[pl.BlockSpec((1, n_h, tq, HEAD_DIM), lambda b, i: (b, 0, i, 0)),
                  full(k), full(k_c), full(v), full(v_c)],
        out_specs=pl.BlockSpec((1, tq, n_h * HEAD_DIM), lambda b, i: (b, i, 0)),
        out_shape=jax.ShapeDtypeStruct((bsz, n_tok, n_h * HEAD_DIM), BF16),
        scratch_shapes=[pltpu.VMEM((tq, n_h * HEAD_DIM), F32)],
        compiler_params=_cparams(("arbitrary", "arbitrary")),
        name="gqa_attention",
    )(q, k, k_c, v, v_c)


def _out1_kernel(x_ref, y_ref, g1_ref, sh2_ref, sc2_ref, wo_ref, lng_ref, lnb_ref, rw_ref,
                 x1_ref, hm_ref, route_ref):
    y = _dot(y_ref[0], wo_ref[...])
    x1 = _layer_norm(DEEPNORM_ALPHA * x_ref[0] + g1_ref[0] * y, lng_ref[...], lnb_ref[...])
    x1_ref[0] = x1
    hm = x1 * (1.0 + sc2_ref[0]) + sh2_ref[0]
    hm_ref[0] = hm
    lane = lax.broadcasted_iota(jnp.int32, (hm.shape[0], LANES), 1)
    logits = jnp.where(lane < N_EXPERTS, _dot3(hm, rw_ref[...]), NEG_INF)
    v1 = logits.max(axis=-1, keepdims=True)
    i1 = jnp.where(logits == v1, lane, LANES).min(axis=-1, keepdims=True)
    rest = jnp.where(lane == i1, NEG_INF, logits)
    v2 = rest.max(axis=-1, keepdims=True)
    i2 = jnp.where(rest == v2, lane, LANES).min(axis=-1, keepdims=True)
    e = jnp.exp(v2 - v1)
    inv = 1.0 / (1.0 + e)
    route_ref[0] = jnp.where(lane == 0, i1.astype(F32),
                             jnp.where(lane == 1, i2.astype(F32),
                                       jnp.where(lane == 2, inv, jnp.where(lane == 3, e * inv, 0.0))))


def _out1(x, y, g1, sh2, sc2, wo, lng, lnb, rw, *, tm):
    bsz, n_tok, _ = x.shape
    mod = pl.BlockSpec((1, 1, D_MODEL), lambda b, i: (b, 0, 0))
    tile = pl.BlockSpec((1, tm, D_MODEL), lambda b, i: (b, i, 0))
    const = lambda a: pl.BlockSpec(a.shape, lambda b, i: (0,) * a.ndim)
    vec = lambda a: a.reshape(1, D_MODEL)
    return pl.pallas_call(
        _out1_kernel,
        grid=(bsz, n_tok // tm),
        in_specs=[tile, tile, mod, mod, mod, const(wo), const(vec(lng)), const(vec(lnb)), const(rw)],
        out_specs=[tile, tile, pl.BlockSpec((1, tm, LANES), lambda b, i: (b, i, 0))],
        out_shape=[jax.ShapeDtypeStruct(x.shape, F32), jax.ShapeDtypeStruct(x.shape, F32),
                   jax.ShapeDtypeStruct((bsz, n_tok, LANES), F32)],
        compiler_params=_cparams(("arbitrary", "arbitrary")),
        name="out1",
    )(x, y, g1, sh2, sc2, wo, vec(lng), vec(lnb), rw)


def _row_copy(src_ref, src_row, dst_ref, dst_row, sem):
    return pltpu.make_async_copy(src_ref.at[pl.ds(src_row, 1)], dst_ref.at[pl.ds(dst_row, 1)], sem)


def _dispatch_kernel(dest_ref, h_ref, xs_in_ref, xs_ref, sem):
    del xs_in_ref
    n_rows = h_ref.shape[0]

    def start(r, carry):
        for k in range(TOP_K):
            _row_copy(h_ref, r, xs_ref, dest_ref[0, 0, TOP_K * r + k], sem).start()
        return carry

    def wait(r, carry):
        for k in range(TOP_K):
            _row_copy(h_ref, 0, xs_ref, 0, sem).wait()
        return carry

    lax.fori_loop(0, n_rows, start, 0)
    lax.fori_loop(0, n_rows, wait, 0)


def _dispatch(hm, dest, n_slots, *, tm):
    n_tok = hm.shape[0]
    dest3 = dest.reshape(n_tok // tm, 1, TOP_K * tm)
    xs0 = jnp.zeros((n_slots, D_MODEL), F32)
    return pl.pallas_call(
        _dispatch_kernel,
        grid=(n_tok // tm,),
        in_specs=[pl.BlockSpec((1, 1, TOP_K * tm), lambda i: (i, 0, 0), memory_space=pltpu.SMEM),
                  pl.BlockSpec((tm, D_MODEL), lambda i: (i, 0)),
                  pl.BlockSpec(memory_space=pl.ANY)],
        out_specs=pl.BlockSpec(memory_space=pl.ANY),
        out_shape=jax.ShapeDtypeStruct((n_slots, D_MODEL), F32),
        scratch_shapes=[pltpu.SemaphoreType.DMA],
        input_output_aliases={2: 0},
        compiler_params=_cparams(("arbitrary",)),
        name="moe_dispatch",
    )(dest3, hm, xs0)


def _moe_kernel(be_ref, nv_ref, xs_ref, w1_ref, w3_ref, w2_ref, o_ref, xb_ref, acc_ref):
    i, j = pl.program_id(0), pl.program_id(1)
    valid = i < nv_ref[0]

    @pl.when(jnp.logical_and(valid, j == 0))
    def _():
        xb_ref[...] = xs_ref[...].astype(BF16)
        acc_ref[...] = jnp.zeros_like(acc_ref)

    @pl.when(valid)
    def _():
        a = _dot(xb_ref[...], w1_ref[0])
        g = _dot(xb_ref[...], w3_ref[0])
        acc_ref[...] += _dot((_silu(a) * g).astype(BF16), w2_ref[0])

    @pl.when(j == pl.num_programs(1) - 1)
    def _():
        o_ref[...] = jnp.where(valid, acc_ref[...], 0.0)


def _moe_experts(xs, block_e, n_valid, w13, w2):
    n_slots = xs.shape[0]
    n_blocks = n_slots // MOE_TM
    nf = EXPERT_DIM // MOE_TF
    row_map = lambda i, j, be, nv: (jnp.minimum(i, nv[0] - 1), 0)
    return pl.pallas_call(
        _moe_kernel,
        grid_spec=pltpu.PrefetchScalarGridSpec(
            num_scalar_prefetch=2,
            grid=(n_blocks, nf),
            in_specs=[pl.BlockSpec((MOE_TM, D_MODEL), row_map),
                      pl.BlockSpec((1, D_MODEL, MOE_TF), lambda i, j, be, nv: (be[i], 0, j)),
                      pl.BlockSpec((1, D_MODEL, MOE_TF), lambda i, j, be, nv: (be[i], 0, nf + j)),
                      pl.BlockSpec((1, MOE_TF, D_MODEL), lambda i, j, be, nv: (be[i], j, 0))],
            out_specs=pl.BlockSpec((MOE_TM, D_MODEL), lambda i, j, be, nv: (i, 0)),
            scratch_shapes=[pltpu.VMEM((MOE_TM, D_MODEL), BF16), pltpu.VMEM((MOE_TM, D_MODEL), F32)]),
        out_shape=jax.ShapeDtypeStruct((n_slots, D_MODEL), F32),
        compiler_params=_cparams(("arbitrary", "arbitrary")),
        name="moe_experts",
    )(block_e, n_valid, xs, w13, w13, w2)


def _combine_kernel(dest_ref, x1_ref, route_ref, g2_ref, lng_ref, lnb_ref, y_ref, o_ref, buf_ref, sem):
    n_rows = x1_ref.shape[0]

    def start(r, carry):
        for k in range(TOP_K):
            _row_copy(y_ref, dest_ref[0, 0, TOP_K * r + k], buf_ref.at[k], r, sem).start()
        return carry

    def wait(r, carry):
        for k in range(TOP_K):
            _row_copy(y_ref, 0, buf_ref.at[k], 0, sem).wait()
        return carry

    lax.fori_loop(0, n_rows, start, 0)
    lax.fori_loop(0, n_rows, wait, 0)
    route = route_ref[...]
    m = buf_ref[0] * route[:, 2:3] + buf_ref[1] * route[:, 3:4]
    o_ref[...] = _layer_norm(DEEPNORM_ALPHA * x1_ref[...] + g2_ref[0] * m, lng_ref[...], lnb_ref[...])


def _combine(x1, route, dest, g2, lng, lnb, y, *, tm, tiles_per_batch):
    n_tok = x1.shape[0]
    dest3 = dest.reshape(n_tok // tm, 1, TOP_K * tm)
    vec = lambda a: a.reshape(1, D_MODEL)
    const = lambda a: pl.BlockSpec(a.shape, lambda i: (0,) * a.ndim)
    return pl.pallas_call(
        _combine_kernel,
        grid=(n_tok // tm,),
        in_specs=[pl.BlockSpec((1, 1, TOP_K * tm), lambda i: (i, 0, 0), memory_space=pltpu.SMEM),
                  pl.BlockSpec((tm, D_MODEL), lambda i: (i, 0)),
                  pl.BlockSpec((tm, LANES), lambda i: (i, 0)),
                  pl.BlockSpec((1, 1, D_MODEL), lambda i: (i // tiles_per_batch, 0, 0)),
                  const(vec(lng)), const(vec(lnb)),
                  pl.BlockSpec(memory_space=pl.ANY)],
        out_specs=pl.BlockSpec((tm, D_MODEL), lambda i: (i, 0)),
        out_shape=jax.ShapeDtypeStruct(x1.shape, F32),
        scratch_shapes=[pltpu.VMEM((TOP_K, tm, D_MODEL), F32), pltpu.SemaphoreType.DMA],
        compiler_params=_cparams(("arbitrary",)),
        name="moe_combine",
    )(dest3, x1, route, g2, vec(lng), vec(lnb), y)


def _routing_plan(route, n_tok):
    flat_e = route[:, :TOP_K].astype(jnp.int32).reshape(-1)
    onehot = (flat_e[:, None] == jnp.arange(N_EXPERTS, dtype=jnp.int32)[None, :]).astype(jnp.int32)
    csum = jnp.cumsum(onehot, axis=0)
    rank = jnp.sum(csum * onehot, axis=1) - 1
    counts = csum[-1]
    padded = (counts + MOE_TM - 1) // MOE_TM * MOE_TM
    pad_end = jnp.cumsum(padded)
    pad_start = pad_end - padded
    dest = jnp.sum(onehot * pad_start[None, :], axis=1) + rank
    n_blocks = -(-(n_tok * TOP_K + N_EXPERTS * (MOE_TM - 1)) // MOE_TM)
    starts = jnp.arange(n_blocks, dtype=jnp.int32) * MOE_TM
    block_e = jnp.minimum(jnp.sum((pad_end[None, :] <= starts[:, None]).astype(jnp.int32), axis=1), N_EXPERTS - 1)
    n_valid = (pad_end[-1] // MOE_TM).astype(jnp.int32).reshape(1)
    return dest.astype(jnp.int32), block_e.astype(jnp.int32), n_valid, n_blocks * MOE_TM


def _rope_tables(n_tok, n_identity=0):
    t = jnp.arange(n_tok, dtype=jnp.int32)
    n_freq = HEAD_DIM // 4
    inv_freq = ROPE_THETA ** (-jnp.arange(n_freq, dtype=F32) / n_freq)
    ang = jnp.concatenate([(t // GRID_W).astype(F32)[:, None] * inv_freq,
                           (t % GRID_W).astype(F32)[:, None] * inv_freq], axis=-1)
    cos, sin = jnp.cos(ang), jnp.sin(ang)
    cos128 = jnp.tile(cos, (1, 4))
    sin128 = jnp.tile(jnp.concatenate([-sin, sin], axis=-1), (1, 2))
    return cos128, sin128


def _mods(cond_rows, ada_w, ada_b, bsz):
    out = _ada_mods(cond_rows, ada_w, ada_b)
    lat = [m.reshape(bsz, 1, D_MODEL) for m in jnp.split(out[:bsz], 6, axis=-1)]
    ctx = [m.reshape(1, 1, D_MODEL) for m in jnp.split(out[bsz:bsz + 1], 6, axis=-1)]
    return lat, ctx


def kernel(x, c, ctx, c_ctx, l0_ada_w, l0_ada_b, l0_w_in, l0_rpb, l0_lambda_qk, l0_subln_g, l0_w_out, l0_ln1_g, l0_ln1_b, l0_ffn_w13, l0_ffn_w2, l0_ln2_g, l0_ln2_b, l1_ada_w, l1_ada_b, l1_w_in, l1_q_norm_g, l1_k_norm_g, l1_w_out, l1_ln1_g, l1_ln1_b, l1_router_w, l1_moe_w13, l1_moe_w2, l1_ln2_g, l1_ln2_b):
    bsz, n_tok, _ = x.shape
    n_ctx = ctx.shape[1]
    cond_rows = jnp.concatenate([c, c_ctx[None, :], jnp.zeros((32 - bsz - 1, D_MODEL), F32)], axis=0)
    cos, sin = _rope_tables(n_tok)
    ones = jnp.ones((n_ctx, LANES), F32)
    zeros = jnp.zeros((n_ctx, LANES), F32)

    (sh1, sc1, g1, sh2, sc2, g2), (csh1, csc1, cg1, csh2, csc2, cg2) = _mods(cond_rows, l0_ada_w, l0_ada_b, bsz)
    w_in0 = l0_w_in.astype(BF16)
    qa, ka, va, qb, kb, vb = _inproj0(x, sh1, sc1, w_in0, cos, sin, tm=512, rope=True)
    qa_c, ka_c, va_c, qb_c, kb_c, vb_c = _inproj0(ctx, csh1, csc1, w_in0, ones, zeros, tm=n_ctx, rope=False)
    lam_init = 0.8 - 0.6 * math.exp(-0.3 * 0)
    lq = l0_lambda_qk.astype(F32)
    lam = (jnp.exp(jnp.sum(lq[0] * lq[1])) - jnp.exp(jnp.sum(lq[2] * lq[3])) + lam_init).reshape(1, 1)
    y_a = _na_attention(qa, ka, va, ka_c, va_c, _na_bias_table(l0_rpb))
    y_b = _diff_attention(lam, qb, kb, vb, kb_c, vb_c, l0_subln_g, tq=256, out_scale=1.0 - lam_init)
    ya_c, yb_c = _ctx_attention0(lam, qa_c, ka_c, va_c, qb_c, kb_c, vb_c, l0_subln_g, out_scale=1.0 - lam_init)
    wo0, w13_0, w2_0 = l0_w_out.astype(BF16), l0_ffn_w13.astype(BF16), l0_ffn_w2.astype(BF16)
    ln0 = (l0_ln1_g, l0_ln1_b, l0_ln2_g, l0_ln2_b)
    x = _post0(x, y_a, y_b, g1, sh2, sc2, g2, wo0, w13_0, w2_0, *ln0, tm=512)
    ctx = _post0(ctx, ya_c, yb_c, cg1, csh2, csc2, cg2, wo0, w13_0, w2_0, *ln0, tm=n_ctx)

    (sh1, sc1, g1, sh2, sc2, g2), (csh1, csc1, _, _, _, _) = _mods(cond_rows, l1_ada_w, l1_ada_b, bsz)
    w_in1 = l1_w_in.astype(BF16)
    n_q, n_kv = GQA_Q_HEADS * HEAD_DIM, GQA_KV_HEADS * HEAD_DIM
    gain = jnp.concatenate([jnp.tile(l1_q_norm_g, GQA_Q_HEADS) * QK_SCALE,
                            jnp.tile(l1_k_norm_g, GQA_KV_HEADS)]).reshape(1, n_q + n_kv)
    q, k, v = _inproj1(x, sh1, sc1, w_in1, gain, cos, sin, tm=512,
                       group_counts=(GQA_Q_HEADS, GQA_KV_HEADS, GQA_KV_HEADS),
                       n_norm_chunks=(n_q + n_kv) // NORM_CHUNK, rope=True)
    k_c, v_c = _inproj1(ctx, csh1, csc1, w_in1[:, n_q:], gain[:, n_q:], ones, zeros, tm=n_ctx,
                        group_counts=(GQA_KV_HEADS, GQA_KV_HEADS), n_norm_chunks=n_kv // NORM_CHUNK, rope=False)
    y = _gqa_attention(q, k, v, k_c, v_c, tq=256)
    rw = jnp.pad(l1_router_w.astype(F32), ((0, 0), (0, LANES - N_EXPERTS)))
    x1, hm, route = _out1(x, y, g1, sh2, sc2, l1_w_out.astype(BF16), l1_ln1_g, l1_ln1_b, rw, tm=512)
    n_all = bsz * n_tok
    x1, hm, route = x1.reshape(n_all, D_MODEL), hm.reshape(n_all, D_MODEL), route.reshape(n_all, LANES)
    dest, block_e, n_valid, n_slots = _routing_plan(route, n_all)
    xs = _dispatch(hm, dest, n_slots, tm=512)
    y_slots = _moe_experts(xs, block_e, n_valid, l1_moe_w13.astype(BF16), l1_moe_w2.astype(BF16))
    out = _combine(x1, route, dest, g2, l1_ln2_g, l1_ln2_b, y_slots, tm=256, tiles_per_batch=n_tok // 256)
    return out.reshape(bsz, n_tok, D_MODEL)
```

```python
import functools
import math

import jax
import jax.numpy as jnp
import numpy as np
from jax import lax
from jax.experimental import pallas as pl
from jax.experimental.pallas import tpu as pltpu

F32 = jnp.float32
BF16 = jnp.bfloat16

D_MODEL = 1024
DEPTH = 2
GRID_W = 64
HEAD_DIM = 64
ROPE_THETA = 10000.0
NA_HEADS = 8
NA_WIN_H = 8
NA_WIN_W = 16
DIFF_HEADS = 4
DIFF_VDIM = 2 * HEAD_DIM
GQA_Q_HEADS = 16
GQA_KV_HEADS = 4
GQA_GROUP = GQA_Q_HEADS // GQA_KV_HEADS
FFN_DIM = 2816
N_EXPERTS = 8
TOP_K = 2
EXPERT_DIM = 3584
DEEPNORM_ALPHA = (2 * DEPTH) ** 0.25
LN_EPS = 1e-5
RMS_EPS = 1e-6
NEG_INF = -1e30
QK_SCALE = HEAD_DIM ** -0.5

LANES = 128
VMEM_LIMIT = 56 * 1024 * 1024

NA_TILE_ROWS = 4
NA_KEY_ROWS = 12
NA_TQ = NA_TILE_ROWS * GRID_W
NA_KEY_CHUNK = NA_TQ
NA_N_KEY_CHUNKS = NA_KEY_ROWS * GRID_W // NA_KEY_CHUNK

MOE_TM = 1024
MOE_TF = 896
MOE_NF = EXPERT_DIM // MOE_TF
MOE_ROWS_PER_STEP = MOE_TM // MOE_NF


def _cparams(sem, vmem=VMEM_LIMIT):
    return pltpu.CompilerParams(dimension_semantics=sem, vmem_limit_bytes=vmem)


def _dot(a, b):
    return jnp.dot(a, b, preferred_element_type=F32)


def _dot_t(a, b):
    return lax.dot_general(a, b, (((1,), (1,)), ((), ())), preferred_element_type=F32)


def _split(a):
    hi = a.astype(BF16)
    lo = (a - hi.astype(F32)).astype(BF16)
    return hi, lo


def _dot3(a, b):
    a_hi, a_lo = _split(a)
    b_hi, b_lo = _split(b)
    return _dot(a_hi, b_hi) + (_dot(a_hi, b_lo) + _dot(a_lo, b_hi))


def _dot2(a, b_exact):
    a_hi, a_lo = _split(a)
    return _dot(a_hi, b_exact) + _dot(a_lo, b_exact)


def _layer_norm(z, g, b):
    mu = jnp.mean(z, axis=-1, keepdims=True)
    zc = z - mu
    var = jnp.mean(zc * zc, axis=-1, keepdims=True)
    return zc * lax.rsqrt(var + LN_EPS) * g + b


def _silu(a):
    return a * jax.nn.sigmoid(a)


def _rope128(y, cos, sin_signed, lane_hi):
    swapped = jnp.where(lane_hi, pltpu.roll(y, 32, 1), pltpu.roll(y, 96, 1))
    return y * cos + swapped * sin_signed


def _store_groups(y, refs_iter):
    for g in range(y.shape[1] // HEAD_DIM):
        ref, idx = next(refs_iter)
        ref[0, idx] = y[:, g * HEAD_DIM:(g + 1) * HEAD_DIM].astype(BF16)


def _ada_kernel(c_ref, w_ref, b_ref, o_ref):
    o_ref[...] = _dot3(_silu(c_ref[...]), w_ref[...]) + b_ref[...]


def _ada_mods(cond, w, b):
    rows, tn = cond.shape[0], 1536
    n_out = w.shape[1]
    return pl.pallas_call(
        _ada_kernel,
        grid=(n_out // tn,),
        in_specs=[pl.BlockSpec((rows, D_MODEL), lambda j: (0, 0)),
                  pl.BlockSpec((D_MODEL, tn), lambda j: (0, j)),
                  pl.BlockSpec((1, tn), lambda j: (0, j))],
        out_specs=pl.BlockSpec((rows, tn), lambda j: (0, j)),
        out_shape=jax.ShapeDtypeStruct((rows, n_out), F32),
        compiler_params=_cparams(("arbitrary",)),
        name="ada_mods",
    )(cond, w, b.reshape(1, n_out))


def _inproj0_kernel(x_ref, sh_ref, sc_ref, w_ref, cos_ref, sin_ref,
                    qa_ref, ka_ref, va_ref, qb_ref, kb_ref, vb_ref, *, rope):
    h = (x_ref[0] * (1.0 + sc_ref[0]) + sh_ref[0]).astype(BF16)
    lane_hi = (lax.broadcasted_iota(jnp.int32, (1, LANES), 1) % HEAD_DIM) >= HEAD_DIM // 2
    sec = NA_HEADS * HEAD_DIM
    plan = ((qa_ref, QK_SCALE, False), (ka_ref, None, False), (va_ref, None, False),
            (qb_ref, QK_SCALE, True), (kb_ref, None, True))
    for s, (ref, scale, rotary) in enumerate(plan):
        p = _dot(h, w_ref[:, s * sec:(s + 1) * sec])
        if scale is not None:
            p = p * scale
        for c in range(sec // LANES):
            y = p[:, c * LANES:(c + 1) * LANES]
            if rotary and rope:
                y = _rope128(y, cos_ref[...], sin_ref[...], lane_hi)
            ref[0, 2 * c] = y[:, :HEAD_DIM].astype(BF16)
            ref[0, 2 * c + 1] = y[:, HEAD_DIM:].astype(BF16)
    p = _dot(h, w_ref[:, 5 * sec:6 * sec])
    for hh in range(DIFF_HEADS):
        vb_ref[0, hh] = p[:, hh * DIFF_VDIM:(hh + 1) * DIFF_VDIM].astype(BF16)


def _inproj0(x, shift, scale, w, cos, sin, *, tm, rope):
    bsz, n_tok, _ = x.shape
    per_batch = shift.shape[0] > 1
    mod_map = (lambda b, i: (b, 0, 0)) if per_batch else (lambda b, i: (0, 0, 0))
    hd = lambda n, d: jax.ShapeDtypeStruct((bsz, n, n_tok, d), BF16)
    hspec = lambda n, d: pl.BlockSpec((1, n, tm, d), lambda b, i: (b, 0, i, 0))
    return pl.pallas_call(
        functools.partial(_inproj0_kernel, rope=rope),
        grid=(bsz, n_tok // tm),
        in_specs=[pl.BlockSpec((1, tm, D_MODEL), lambda b, i: (b, i, 0)),
                  pl.BlockSpec((1, 1, D_MODEL), mod_map),
                  pl.BlockSpec((1, 1, D_MODEL), mod_map),
                  pl.BlockSpec(w.shape, lambda b, i: (0, 0)),
                  pl.BlockSpec((tm, LANES), lambda b, i: (i, 0)),
                  pl.BlockSpec((tm, LANES), lambda b, i: (i, 0))],
        out_specs=[hspec(8, 64), hspec(8, 64), hspec(8, 64), hspec(8, 64), hspec(8, 64), hspec(4, 128)],
        out_shape=[hd(8, 64), hd(8, 64), hd(8, 64), hd(8, 64), hd(8, 64), hd(4, 128)],
        compiler_params=_cparams(("arbitrary", "arbitrary")),
        name="inproj0",
    )(x, shift, scale, w, cos, sin)


def _scores(q, ks, biases):
    out = []
    for k, bias in zip(ks, biases):
        s = _dot_t(q, k)
        out.append(s if bias is None else s + bias)
    return out


def _softmax_parts(ss):
    m = ss[0].max(axis=-1, keepdims=True)
    for s in ss[1:]:
        m = jnp.maximum(m, s.max(axis=-1, keepdims=True))
    es = [jnp.exp(s - m) for s in ss]
    l = es[0].sum(axis=-1, keepdims=True)
    for e in es[1:]:
        l = l + e.sum(axis=-1, keepdims=True)
    return es, l


def _softmax_attend(q, ks, vs, biases):
    es, l = _softmax_parts(_scores(q, ks, biases))
    o = _dot(es[0].astype(BF16), vs[0])
    for e, v in zip(es[1:], vs[1:]):
        o = o + _dot(e.astype(BF16), v)
    return o * (1.0 / l)


def _diff_attend(q0, q1, k0s, k1s, vs, lam, subln_g, out_scale):
    e0, l0 = _softmax_parts(_scores(q0, k0s, [None] * len(k0s)))
    e1, l1 = _softmax_parts(_scores(q1, k1s, [None] * len(k1s)))
    r0 = 1.0 / l0
    r1 = lam / l1
    y = None
    for a0, a1, v in zip(e0, e1, vs):
        t = _dot((a0 * r0 - a1 * r1).astype(BF16), v)
        y = t if y is None else y + t
    y = y * lax.rsqrt(jnp.mean(y * y, axis=-1, keepdims=True) + RMS_EPS) * subln_g
    return y * out_scale


def _na_kernel(q_ref, k0_ref, k1_ref, k2_ref, v0_ref, v1_ref, v2_ref, kc_ref, vc_ref, bias_ref,
               o_ref, acc_ref):
    for h in range(NA_HEADS):
        ks = [k0_ref[0, h], k1_ref[0, h], k2_ref[0, h], kc_ref[0, h]]
        vs = [v0_ref[0, h], v1_ref[0, h], v2_ref[0, h], vc_ref[0, h]]
        biases = [bias_ref[0, h, :, j * NA_KEY_CHUNK:(j + 1) * NA_KEY_CHUNK] for j in range(NA_N_KEY_CHUNKS)]
        acc_ref[:, h * HEAD_DIM:(h + 1) * HEAD_DIM] = _softmax_attend(q_ref[0, h], ks, vs, biases + [None])
    o_ref[0] = acc_ref[...].astype(BF16)


def _na_bias_table(rpb):
    rows_total = 32
    n_h = rpb.shape[0]
    pad = GRID_W - NA_WIN_W
    rpb_pad = jnp.pad(rpb.astype(F32), ((0, 0), (0, 0), (pad, pad)))
    col_blocks = jnp.stack([rpb_pad[:, :, pad + NA_WIN_W - 1 - qc:pad + NA_WIN_W - 1 - qc + GRID_W]
                            for qc in range(GRID_W)], axis=2)
    q_c = np.arange(GRID_W)
    col_start = np.clip(q_c - NA_WIN_W // 2, 0, GRID_W - NA_WIN_W)
    col_ok = (q_c[None, :] >= col_start[:, None]) & (q_c[None, :] < col_start[:, None] + NA_WIN_W)
    col_blocks = jnp.where(jnp.asarray(col_ok)[None, None], col_blocks, NEG_INF)
    masked = jnp.full((n_h, GRID_W, GRID_W), NEG_INF, F32)
    kinds = []
    for r0 in (0, 4, 24, 28):
        ws = int(np.clip(r0 - NA_WIN_H // 2, 0, rows_total - NA_KEY_ROWS))
        q_rows = []
        for qr in range(r0, r0 + NA_TILE_ROWS):
            row_start = int(np.clip(qr - NA_WIN_H // 2, 0, rows_total - NA_WIN_H))
            blocks = []
            for kr in range(ws, ws + NA_KEY_ROWS):
                inside = row_start <= kr < row_start + NA_WIN_H
                blocks.append(col_blocks[:, kr - qr + NA_WIN_H - 1] if inside else masked)
            q_rows.append(jnp.concatenate(blocks, axis=-1))
        kinds.append(jnp.concatenate(q_rows, axis=1))
    return jnp.stack(kinds, axis=0)


def _na_attention(qa, ka, va, ka_c, va_c, bias_tab):
    bsz, n_h, n_tok, _ = qa.shape
    n_ctx = ka_c.shape[2]
    n_tiles = n_tok // NA_TQ
    last_start = (n_tok - NA_KEY_ROWS * GRID_W) // NA_KEY_CHUNK

    def win(j):
        return lambda t, b: (b, 0, jnp.clip(t - 1, 0, last_start) + j, 0)

    kv_spec = lambda j: pl.BlockSpec((1, n_h, NA_KEY_CHUNK, HEAD_DIM), win(j))
    ctx_spec = pl.BlockSpec((1, n_h, n_ctx, HEAD_DIM), lambda t, b: (b, 0, 0, 0))
    bias_map = lambda t, b: (jnp.minimum(t, 1) + jnp.maximum(t - (n_tiles - 3), 0), 0, 0, 0)
    return pl.pallas_call(
        _na_kernel,
        grid=(n_tiles, bsz),
        in_specs=[pl.BlockSpec((1, n_h, NA_TQ, HEAD_DIM), lambda t, b: (b, 0, t, 0)),
                  kv_spec(0), kv_spec(1), kv_spec(2), kv_spec(0), kv_spec(1), kv_spec(2),
                  ctx_spec, ctx_spec,
                  pl.BlockSpec((1, n_h, NA_TQ, NA_KEY_ROWS * GRID_W), bias_map)],
        out_specs=pl.BlockSpec((1, NA_TQ, n_h * HEAD_DIM), lambda t, b: (b, t, 0)),
        out_shape=jax.ShapeDtypeStruct((bsz, n_tok, n_h * HEAD_DIM), BF16),
        scratch_shapes=[pltpu.VMEM((NA_TQ, n_h * HEAD_DIM), F32)],
        compiler_params=_cparams(("arbitrary", "arbitrary")),
        name="na_attention",
    )(qa, ka, ka, ka, va, va, va, ka_c, va_c, bias_tab)


def _diff_kernel(lam_ref, q_ref, k_ref, kc_ref, v_ref, vc_ref, g_ref, o_ref, *, out_scale, with_latent):
    lam = lam_ref[0, 0]
    for h in range(DIFF_HEADS):
        if with_latent:
            k0s = [k_ref[0, 2 * h], kc_ref[0, 2 * h]]
            k1s = [k_ref[0, 2 * h + 1], kc_ref[0, 2 * h + 1]]
            vs = [v_ref[0, h], vc_ref[0, h]]
        else:
            k0s, k1s, vs = [kc_ref[0, 2 * h]], [kc_ref[0, 2 * h + 1]], [vc_ref[0, h]]
        y = _diff_attend(q_ref[0, 2 * h], q_ref[0, 2 * h + 1], k0s, k1s, vs, lam, g_ref[...], out_scale)
        o_ref[0, :, h * DIFF_VDIM:(h + 1) * DIFF_VDIM] = y.astype(BF16)


def _diff_attention(lam, qb, kb, vb, kb_c, vb_c, subln_g, *, tq, out_scale):
    bsz, _, n_tok, _ = qb.shape
    n_ctx = kb_c.shape[2]
    full = lambda a: pl.BlockSpec((1,) + a.shape[1:], lambda b, i: (b, 0, 0, 0))
    return pl.pallas_call(
        functools.partial(_diff_kernel, out_scale=out_scale, with_latent=True),
        grid=(bsz, n_tok // tq),
        in_specs=[pl.BlockSpec(memory_space=pltpu.SMEM),
                  pl.BlockSpec((1, 2 * DIFF_HEADS, tq, HEAD_DIM), lambda b, i: (b, 0, i, 0)),
                  full(kb), full(kb_c), full(vb), full(vb_c),
                  pl.BlockSpec((1, DIFF_VDIM), lambda b, i: (0, 0))],
        out_specs=pl.BlockSpec((1, tq, DIFF_HEADS * DIFF_VDIM), lambda b, i: (b, i, 0)),
        out_shape=jax.ShapeDtypeStruct((bsz, n_tok, DIFF_HEADS * DIFF_VDIM), BF16),
        compiler_params=_cparams(("arbitrary", "arbitrary")),
        name="diff_attention",
    )(lam, qb, kb, kb_c, vb, vb_c, subln_g.reshape(1, DIFF_VDIM))


def _ctx_attn0_kernel(lam_ref, qa_ref, ka_ref, va_ref, qb_ref, kb_ref, vb_ref, g_ref, ya_ref, yb_ref, acc_ref,
                      *, out_scale):
    for h in range(NA_HEADS):
        acc_ref[:, h * HEAD_DIM:(h + 1) * HEAD_DIM] = _softmax_attend(
            qa_ref[0, h], [ka_ref[0, h]], [va_ref[0, h]], [None])
    ya_ref[0] = acc_ref[...].astype(BF16)
    _diff_kernel(lam_ref, qb_ref, None, kb_ref, None, vb_ref, g_ref, yb_ref,
                 out_scale=out_scale, with_latent=False)


def _ctx_attention0(lam, qa, ka, va, qb, kb, vb, subln_g, *, out_scale):
    bsz, _, n_ctx, _ = qa.shape
    full = lambda a: pl.BlockSpec((1,) + a.shape[1:], lambda b: (b, 0, 0, 0))
    width = NA_HEADS * HEAD_DIM
    out = jax.ShapeDtypeStruct((bsz, n_ctx, width), BF16)
    return pl.pallas_call(
        functools.partial(_ctx_attn0_kernel, out_scale=out_scale),
        grid=(bsz,),
        in_specs=[pl.BlockSpec(memory_space=pltpu.SMEM),
                  full(qa), full(ka), full(va), full(qb), full(kb), full(vb),
                  pl.BlockSpec((1, DIFF_VDIM), lambda b: (0, 0))],
        out_specs=[pl.BlockSpec((1, n_ctx, width), lambda b: (b, 0, 0))] * 2,
        out_shape=[out, out],
        scratch_shapes=[pltpu.VMEM((n_ctx, width), F32)],
        compiler_params=_cparams(("arbitrary",)),
        name="ctx_attention0",
    )(lam, qa, ka, va, qb, kb, vb, subln_g.reshape(1, DIFF_VDIM))


FFN_CHUNK = FFN_DIM // 2


def _post0_kernel(x_ref, ya_ref, yb_ref, g1_ref, sh2_ref, sc2_ref, g2_ref, wo_ref, w13_ref, w2_ref,
                  ln1g_ref, ln1b_ref, ln2g_ref, ln2b_ref, o_ref):
    half = wo_ref.shape[0] // 2
    y = _dot(ya_ref[0], wo_ref[:half, :]) + _dot(yb_ref[0], wo_ref[half:, :])
    x1 = _layer_norm(DEEPNORM_ALPHA * x_ref[0] + g1_ref[0] * y, ln1g_ref[...], ln1b_ref[...])
    h = (x1 * (1.0 + sc2_ref[0]) + sh2_ref[0]).astype(BF16)
    f = None
    for c in range(FFN_DIM // FFN_CHUNK):
        a = _dot(h, w13_ref[:, c * FFN_CHUNK:(c + 1) * FFN_CHUNK])
        g = _dot(h, w13_ref[:, FFN_DIM + c * FFN_CHUNK:FFN_DIM + (c + 1) * FFN_CHUNK])
        t = _dot((_silu(a) * g).astype(BF16), w2_ref[c * FFN_CHUNK:(c + 1) * FFN_CHUNK, :])
        f = t if f is None else f + t
    o_ref[0] = _layer_norm(DEEPNORM_ALPHA * x1 + g2_ref[0] * f, ln2g_ref[...], ln2b_ref[...])


def _post0(x, ya, yb, g1, sh2, sc2, g2, wo, w13, w2, ln1g, ln1b, ln2g, ln2b, *, tm):
    bsz, n_tok, _ = x.shape
    per_batch = g1.shape[0] > 1
    mod_map = (lambda b, i: (b, 0, 0)) if per_batch else (lambda b, i: (0, 0, 0))
    mod = pl.BlockSpec((1, 1, D_MODEL), mod_map)
    const = lambda a: pl.BlockSpec(a.shape, lambda b, i: (0,) * a.ndim, pipeline_mode=pl.Buffered(1))
    vec = lambda a: a.reshape(1, D_MODEL)
    half = ya.shape[2]
    return pl.pallas_call(
        _post0_kernel,
        grid=(bsz, n_tok // tm),
        in_specs=[pl.BlockSpec((1, tm, D_MODEL), lambda b, i: (b, i, 0)),
                  pl.BlockSpec((1, tm, half), lambda b, i: (b, i, 0)),
                  pl.BlockSpec((1, tm, half), lambda b, i: (b, i, 0)),
                  mod, mod, mod, mod, const(wo), const(w13), const(w2),
                  const(vec(ln1g)), const(vec(ln1b)), const(vec(ln2g)), const(vec(ln2b))],
        out_specs=pl.BlockSpec((1, tm, D_MODEL), lambda b, i: (b, i, 0)),
        out_shape=jax.ShapeDtypeStruct(x.shape, F32),
        compiler_params=_cparams(("arbitrary", "arbitrary")),
        name="post0",
    )(x, ya, yb, g1, sh2, sc2, g2, wo, w13, w2, vec(ln1g), vec(ln1b), vec(ln2g), vec(ln2b))


NORM_CHUNK = 4 * HEAD_DIM


def _inproj1_kernel(x_ref, sh_ref, sc_ref, w_ref, gain_ref, ind_ref, indt_ref, cos_ref, sin_ref, *out_refs,
                    n_norm_chunks, group_counts, rope):
    h = (x_ref[0] * (1.0 + sc_ref[0]) + sh_ref[0]).astype(BF16)
    lane_hi = (lax.broadcasted_iota(jnp.int32, (1, LANES), 1) % HEAD_DIM) >= HEAD_DIM // 2
    targets = iter([(ref, i) for ref, n in zip(out_refs, group_counts) for i in range(n)])
    n_chunks = w_ref.shape[1] // NORM_CHUNK
    for c in range(n_chunks):
        cols = slice(c * NORM_CHUNK, (c + 1) * NORM_CHUNK)
        p = _dot(h, w_ref[:, cols])
        if c < n_norm_chunks:
            ms = _dot2(p * p, ind_ref[...]) * (1.0 / HEAD_DIM)
            r = _dot2(lax.rsqrt(ms + RMS_EPS), indt_ref[...])
            p = p * r * gain_ref[:, cols]
            if rope:
                p = jnp.concatenate(
                    [_rope128(p[:, j * LANES:(j + 1) * LANES], cos_ref[...], sin_ref[...], lane_hi)
                     for j in range(NORM_CHUNK // LANES)], axis=1)
        _store_groups(p, targets)


def _inproj1(x, shift, scale, w, gain, cos, sin, *, tm, group_counts, n_norm_chunks, rope):
    bsz, n_tok, _ = x.shape
    per_batch = shift.shape[0] > 1
    mod_map = (lambda b, i: (b, 0, 0)) if per_batch else (lambda b, i: (0, 0, 0))
    ind_np = np.zeros((NORM_CHUNK, LANES), np.float32)
    ind_np[np.arange(NORM_CHUNK), np.arange(NORM_CHUNK) // HEAD_DIM] = 1.0
    ind = jnp.asarray(ind_np, BF16)
    indt = jnp.asarray(ind_np.T, BF16)
    const = lambda a: pl.BlockSpec(a.shape, lambda b, i: (0,) * a.ndim)
    return pl.pallas_call(
        functools.partial(_inproj1_kernel, n_norm_chunks=n_norm_chunks, group_counts=group_counts, rope=rope),
        grid=(bsz, n_tok // tm),
        in_specs=[pl.BlockSpec((1, tm, D_MODEL), lambda b, i: (b, i, 0)),
                  pl.BlockSpec((1, 1, D_MODEL), mod_map),
                  pl.BlockSpec((1, 1, D_MODEL), mod_map),
                  const(w), const(gain), const(ind), const(indt),
                  pl.BlockSpec((tm, LANES), lambda b, i: (i, 0)),
                  pl.BlockSpec((tm, LANES), lambda b, i: (i, 0))],
        out_specs=[pl.BlockSpec((1, n, tm, HEAD_DIM), lambda b, i: (b, 0, i, 0)) for n in group_counts],
        out_shape=[jax.ShapeDtypeStruct((bsz, n, n_tok, HEAD_DIM), BF16) for n in group_counts],
        compiler_params=_cparams(("arbitrary", "arbitrary")),
        name="inproj1",
    )(x, shift, scale, w, gain, ind, indt, cos, sin)


def _gqa_kernel(q_ref, k_ref, kc_ref, v_ref, vc_ref, o_ref, acc_ref):
    for h in range(GQA_Q_HEADS):
        g = h // GQA_GROUP
        acc_ref[:, h * HEAD_DIM:(h + 1) * HEAD_DIM] = _softmax_attend(
            q_ref[0, h], [k_ref[0, g], kc_ref[0, g]], [v_ref[0, g], vc_ref[0, g]], [None, None])
    o_ref[0] = acc_ref[...].astype(BF16)


def _gqa_attention(q, k, v, k_c, v_c, *, tq):
    bsz, n_h, n_tok, _ = q.shape
    full = lambda a: pl.BlockSpec((1,) + a.shape[1:], lambda b, i: (b, 0, 0, 0))
    return pl.pallas_call(
        _gqa_kernel,
        grid=(bsz, n_tok // tq),
        in_specs=[pl.BlockSpec((1, n_h, tq, HEAD_DIM), lambda b, i: (b, 0, i, 0)),
                  full(k), full(k_c), full(v), full(v_c)],
        out_specs=pl.BlockSpec((1, tq, n_h * HEAD_DIM), lambda b, i: (b, i, 0)),
        out_shape=jax.ShapeDtypeStruct((bsz, n_tok, n_h * HEAD_DIM), BF16),
        scratch_shapes=[pltpu.VMEM((tq, n_h * HEAD_DIM), F32)],
        compiler_params=_cparams(("arbitrary", "arbitrary")),
        name="gqa_attention",
    )(q, k, k_c, v, v_c)


def _out1_kernel(x_ref, y_ref, g1_ref, sh2_ref, sc2_ref, wo_ref, lng_ref, lnb_ref, rw_ref,
                 x1_ref, hm_ref, route_ref):
    y = _dot(y_ref[0], wo_ref[...])
    x1 = _layer_norm(DEEPNORM_ALPHA * x_ref[0] + g1_ref[0] * y, lng_ref[...], lnb_ref[...])
    x1_ref[0] = x1
    hm = x1 * (1.0 + sc2_ref[0]) + sh2_ref[0]
    hm_ref[0] = hm
    lane = lax.broadcasted_iota(jnp.int32, (hm.shape[0], LANES), 1)
    logits = jnp.where(lane < N_EXPERTS, _dot3(hm, rw_ref[...]), NEG_INF)
    v1 = logits.max(axis=-1, keepdims=True)
    i1 = jnp.where(logits == v1, lane, LANES).min(axis=-1, keepdims=True)
    rest = jnp.where(lane == i1, NEG_INF, logits)
    v2 = rest.max(axis=-1, keepdims=True)
    i2 = jnp.where(rest == v2, lane, LANES).min(axis=-1, keepdims=True)
    e = jnp.exp(v2 - v1)
    inv = 1.0 / (1.0 + e)
    route_ref[0] = jnp.where(lane == 0, i1.astype(F32),
                             jnp.where(lane == 1, i2.astype(F32),
                                       jnp.where(lane == 2, inv, jnp.where(lane == 3, e * inv, 0.0))))


def _out1(x, y, g1, sh2, sc2, wo, lng, lnb, rw, *, tm):
    bsz, n_tok, _ = x.shape
    mod = pl.BlockSpec((1, 1, D_MODEL), lambda b, i: (b, 0, 0))
    tile = pl.BlockSpec((1, tm, D_MODEL), lambda b, i: (b, i, 0))
    const = lambda a: pl.BlockSpec(a.shape, lambda b, i: (0,) * a.ndim)
    vec = lambda a: a.reshape(1, D_MODEL)
    return pl.pallas_call(
        _out1_kernel,
        grid=(bsz, n_tok // tm),
        in_specs=[tile, tile, mod, mod, mod, const(wo), const(vec(lng)), const(vec(lnb)), const(rw)],
        out_specs=[tile, tile, pl.BlockSpec((1, tm, LANES), lambda b, i: (b, i, 0))],
        out_shape=[jax.ShapeDtypeStruct(x.shape, F32), jax.ShapeDtypeStruct(x.shape, F32),
                   jax.ShapeDtypeStruct((bsz, n_tok, LANES), F32)],
        compiler_params=_cparams(("arbitrary", "arbitrary")),
        name="out1",
    )(x, y, g1, sh2, sc2, wo, vec(lng), vec(lnb), rw)


def _moe_kernel(be_ref, nv_ref, gfirst_ref, gnext_ref, sprev_ref, slast_ref, hm_ref, w1_ref, w3_ref, w2_ref,
                y_ref, xg_ref, xb_ref, acc_ref, gsem, ssem):
    del be_ref
    i, j = pl.program_id(0), pl.program_id(1)
    n_valid = nv_ref[0]
    slot = i % 2
    other = 1 - slot

    def gather_row(idx_ref, row, dst_slot):
        return pltpu.make_async_copy(hm_ref.at[pl.ds(idx_ref[0, 0, row], 1)],
                                     xg_ref.at[dst_slot, pl.ds(row, 1)], gsem.at[dst_slot])

    def scatter_row(idx_ref, row, src_slot):
        return pltpu.make_async_copy(acc_ref.at[src_slot, pl.ds(row, 1)],
                                     y_ref.at[pl.ds(idx_ref[0, 0, row], 1)], ssem.at[src_slot])

    def wait_gather(s):
        pltpu.make_async_copy(hm_ref.at[pl.ds(0, MOE_TM)], xg_ref.at[s], gsem.at[s]).wait()

    def wait_scatter(s):
        pltpu.make_async_copy(acc_ref.at[s], y_ref.at[pl.ds(0, MOE_TM)], ssem.at[s]).wait()

    @pl.when(jnp.logical_and(i == 0, j == 0))
    def _():
        acc_ref[1] = jnp.zeros((MOE_TM, D_MODEL), F32)

        def start(r, carry):
            gather_row(gfirst_ref, r, 0).start()
            return carry

        lax.fori_loop(0, MOE_TM, start, 0)
        n_real = y_ref.shape[0] - 2 * MOE_TM
        fill = pltpu.make_async_copy(acc_ref.at[1], y_ref.at[pl.ds(n_real, MOE_TM)], ssem.at[1])
        fill.start()
        fill.wait()

    @pl.when(jnp.logical_and(i < n_valid, j == 0))
    def _():
        wait_gather(slot)
        xb_ref[...] = xg_ref[slot].astype(BF16)

        @pl.when(i >= 1)
        def _():
            wait_scatter(slot)

        acc_ref[slot] = jnp.zeros((MOE_TM, D_MODEL), F32)

    @pl.when(i < n_valid)
    def _():
        base = j * MOE_ROWS_PER_STEP
        for r in range(MOE_ROWS_PER_STEP):
            gather_row(gnext_ref, base + r, other).start()
            scatter_row(sprev_ref, base + r, other).start()
        a = _dot(xb_ref[...], w1_ref[0])
        g = _dot(xb_ref[...], w3_ref[0])
        acc_ref[slot] += _dot((_silu(a) * g).astype(BF16), w2_ref[0])

    @pl.when(jnp.logical_and(i == n_valid - 1, j == MOE_NF - 1))
    def _():
        def start(r, carry):
            scatter_row(slast_ref, r, slot).start()
            return carry

        lax.fori_loop(0, MOE_TM, start, 0)
        wait_gather(other)
        wait_scatter(other)
        wait_scatter(slot)


def _moe_experts(hm_packed, plan, w13, w2):
    block_e, n_valid, gidx, sidx, n_out_rows = plan
    n_blocks = block_e.shape[0]
    smem_rows = lambda index_map: pl.BlockSpec((1, 1, MOE_TM), index_map, memory_space=pltpu.SMEM)
    return pl.pallas_call(
        _moe_kernel,
        grid_spec=pltpu.PrefetchScalarGridSpec(
            num_scalar_prefetch=2,
            grid=(n_blocks, MOE_NF),
            in_specs=[smem_rows(lambda i, j, be, nv: (0, 0, 0)),
                      smem_rows(lambda i, j, be, nv: (i + 1, 0, 0)),
                      smem_rows(lambda i, j, be, nv: (i, 0, 0)),
                      smem_rows(lambda i, j, be, nv: (nv[0], 0, 0)),
                      pl.BlockSpec(memory_space=pl.ANY),
                      pl.BlockSpec((1, D_MODEL, MOE_TF), lambda i, j, be, nv: (be[i], 0, j)),
                      pl.BlockSpec((1, D_MODEL, MOE_TF), lambda i, j, be, nv: (be[i], 0, MOE_NF + j)),
                      pl.BlockSpec((1, MOE_TF, D_MODEL), lambda i, j, be, nv: (be[i], j, 0))],
            out_specs=pl.BlockSpec(memory_space=pl.ANY),
            scratch_shapes=[pltpu.VMEM((2, MOE_TM, D_MODEL), F32),
                            pltpu.VMEM((MOE_TM, D_MODEL), BF16),
                            pltpu.VMEM((2, MOE_TM, D_MODEL), F32),
                            pltpu.SemaphoreType.DMA((2,)),
                            pltpu.SemaphoreType.DMA((2,))]),
        out_shape=jax.ShapeDtypeStruct((n_out_rows, D_MODEL), F32),
        compiler_params=_cparams(("arbitrary", "arbitrary")),
        name="moe_experts",
    )(block_e, n_valid, gidx, gidx, sidx, sidx, hm_packed, w13, w13, w2)


def _combine_kernel(x1_ref, y_ref, route_ref, g2_ref, lng_ref, lnb_ref, o_ref):
    route = route_ref[...]
    m = y_ref[:, :D_MODEL] * route[:, 2:3] + y_ref[:, D_MODEL:] * route[:, 3:4]
    o_ref[...] = _layer_norm(DEEPNORM_ALPHA * x1_ref[...] + g2_ref[0] * m, lng_ref[...], lnb_ref[...])


def _combine(x1, y_pairs, route, g2, lng, lnb, *, tm, tiles_per_batch):
    n_tok = x1.shape[0]
    vec = lambda a: a.reshape(1, D_MODEL)
    const = lambda a: pl.BlockSpec(a.shape, lambda i: (0,) * a.ndim)
    return pl.pallas_call(
        _combine_kernel,
        grid=(n_tok // tm,),
        in_specs=[pl.BlockSpec((tm, D_MODEL), lambda i: (i, 0)),
                  pl.BlockSpec((tm, TOP_K * D_MODEL), lambda i: (i, 0)),
                  pl.BlockSpec((tm, LANES), lambda i: (i, 0)),
                  pl.BlockSpec((1, 1, D_MODEL), lambda i: (i // tiles_per_batch, 0, 0)),
                  const(vec(lng)), const(vec(lnb))],
        out_specs=pl.BlockSpec((tm, D_MODEL), lambda i: (i, 0)),
        out_shape=jax.ShapeDtypeStruct(x1.shape, F32),
        compiler_params=_cparams(("arbitrary",)),
        name="moe_combine",
    )(x1, y_pairs, route, g2, vec(lng), vec(lnb))


def _routing_plan(route, n_tok):
    n_pairs = n_tok * TOP_K
    flat_e = route[:, :TOP_K].astype(jnp.int32).reshape(-1)
    pair = jnp.arange(n_pairs, dtype=jnp.int32)
    sorted_key = jnp.sort(flat_e * n_pairs + pair)
    order = sorted_key - (sorted_key // n_pairs) * n_pairs
    experts = jnp.arange(N_EXPERTS, dtype=jnp.int32)
    counts = jnp.sum((flat_e[:, None] == experts[None, :]).astype(jnp.int32), axis=0)
    first_pair = jnp.cumsum(counts) - counts
    blocks_per = (counts + MOE_TM - 1) // MOE_TM
    block_end = jnp.cumsum(blocks_per)
    block_start = block_end - blocks_per
    n_blocks = -(-(n_pairs + N_EXPERTS * (MOE_TM - 1)) // MOE_TM)
    blk = jnp.arange(n_blocks, dtype=jnp.int32)
    block_e = jnp.minimum(jnp.sum((block_end[None, :] <= blk[:, None]).astype(jnp.int32), axis=1), N_EXPERTS - 1)
    n_valid = block_end[-1]
    sel = (block_e[:, None] == experts[None, :]).astype(jnp.int32)
    pick = lambda v: jnp.sum(sel * v[None, :], axis=1)
    offset = (blk - pick(block_start)) * MOE_TM
    src = jnp.clip(pick(first_pair) + offset, 0, n_pairs)
    order_pad = jnp.concatenate([order, jnp.zeros((MOE_TM,), jnp.int32)])
    pairs = jax.vmap(lambda s: lax.dynamic_slice(order_pad, (s,), (MOE_TM,)))(src)
    r = jnp.arange(MOE_TM, dtype=jnp.int32)[None, :]
    used = jnp.logical_and((blk < n_valid)[:, None], offset[:, None] + r < pick(counts)[:, None])
    spare = n_pairs + (blk % 2)[:, None] * MOE_TM + r
    gidx = jnp.where(used, pairs // TOP_K, 0)
    sidx = jnp.where(used, pairs, spare)
    gidx = jnp.concatenate([gidx, jnp.zeros((1, MOE_TM), jnp.int32)], axis=0)
    sidx = jnp.concatenate([n_pairs + MOE_TM + r, sidx], axis=0)
    shape3 = (n_blocks + 1, 1, MOE_TM)
    return (block_e.astype(jnp.int32), n_valid.astype(jnp.int32).reshape(1), gidx.reshape(shape3),
            sidx.reshape(shape3), n_pairs + 2 * MOE_TM)


def _rope_tables(n_tok, n_identity=0):
    t = jnp.arange(n_tok, dtype=jnp.int32)
    n_freq = HEAD_DIM // 4
    inv_freq = ROPE_THETA ** (-jnp.arange(n_freq, dtype=F32) / n_freq)
    ang = jnp.concatenate([(t // GRID_W).astype(F32)[:, None] * inv_freq,
                           (t % GRID_W).astype(F32)[:, None] * inv_freq], axis=-1)
    cos, sin = jnp.cos(ang), jnp.sin(ang)
    cos128 = jnp.tile(cos, (1, 4))
    sin128 = jnp.tile(jnp.concatenate([-sin, sin], axis=-1), (1, 2))
    return cos128, sin128


def _mods(cond_rows, ada_w, ada_b, bsz):
    out = _ada_mods(cond_rows, ada_w, ada_b)
    lat = [m.reshape(bsz, 1, D_MODEL) for m in jnp.split(out[:bsz], 6, axis=-1)]
    ctx = [m.reshape(1, 1, D_MODEL) for m in jnp.split(out[bsz:bsz + 1], 6, axis=-1)]
    return lat, ctx


def kernel(x, c, ctx, c_ctx, l0_ada_w, l0_ada_b, l0_w_in, l0_rpb, l0_lambda_qk, l0_subln_g, l0_w_out, l0_ln1_g, l0_ln1_b, l0_ffn_w13, l0_ffn_w2, l0_ln2_g, l0_ln2_b, l1_ada_w, l1_ada_b, l1_w_in, l1_q_norm_g, l1_k_norm_g, l1_w_out, l1_ln1_g, l1_ln1_b, l1_router_w, l1_moe_w13, l1_moe_w2, l1_ln2_g, l1_ln2_b):
    bsz, n_tok, _ = x.shape
    n_ctx = ctx.shape[1]
    cond_rows = jnp.concatenate([c, c_ctx[None, :], jnp.zeros((32 - bsz - 1, D_MODEL), F32)], axis=0)
    cos, sin = _rope_tables(n_tok)
    ones = jnp.ones((n_ctx, LANES), F32)
    zeros = jnp.zeros((n_ctx, LANES), F32)

    (sh1, sc1, g1, sh2, sc2, g2), (csh1, csc1, cg1, csh2, csc2, cg2) = _mods(cond_rows, l0_ada_w, l0_ada_b, bsz)
    w_in0 = l0_w_in.astype(BF16)
    qa, ka, va, qb, kb, vb = _inproj0(x, sh1, sc1, w_in0, cos, sin, tm=512, rope=True)
    qa_c, ka_c, va_c, qb_c, kb_c, vb_c = _inproj0(ctx, csh1, csc1, w_in0, ones, zeros, tm=n_ctx, rope=False)
    lam_init = 0.8 - 0.6 * math.exp(-0.3 * 0)
    lq = l0_lambda_qk.astype(F32)
    lam = (jnp.exp(jnp.sum(lq[0] * lq[1])) - jnp.exp(jnp.sum(lq[2] * lq[3])) + lam_init).reshape(1, 1)
    y_a = _na_attention(qa, ka, va, ka_c, va_c, _na_bias_table(l0_rpb))
    y_b = _diff_attention(lam, qb, kb, vb, kb_c, vb_c, l0_subln_g, tq=256, out_scale=1.0 - lam_init)
    ya_c, yb_c = _ctx_attention0(lam, qa_c, ka_c, va_c, qb_c, kb_c, vb_c, l0_subln_g, out_scale=1.0 - lam_init)
    wo0, w13_0, w2_0 = l0_w_out.astype(BF16), l0_ffn_w13.astype(BF16), l0_ffn_w2.astype(BF16)
    ln0 = (l0_ln1_g, l0_ln1_b, l0_ln2_g, l0_ln2_b)
    x = _post0(x, y_a, y_b, g1, sh2, sc2, g2, wo0, w13_0, w2_0, *ln0, tm=512)
    ctx = _post0(ctx, ya_c, yb_c, cg1, csh2, csc2, cg2, wo0, w13_0, w2_0, *ln0, tm=n_ctx)

    (sh1, sc1, g1, sh2, sc2, g2), (csh1, csc1, _, _, _, _) = _mods(cond_rows, l1_ada_w, l1_ada_b, bsz)
    w_in1 = l1_w_in.astype(BF16)
    n_q, n_kv = GQA_Q_HEADS * HEAD_DIM, GQA_KV_HEADS * HEAD_DIM
    gain = jnp.concatenate([jnp.tile(l1_q_norm_g, GQA_Q_HEADS) * QK_SCALE,
                            jnp.tile(l1_k_norm_g, GQA_KV_HEADS)]).reshape(1, n_q + n_kv)
    q, k, v = _inproj1(x, sh1, sc1, w_in1, gain, cos, sin, tm=512,
                       group_counts=(GQA_Q_HEADS, GQA_KV_HEADS, GQA_KV_HEADS),
                       n_norm_chunks=(n_q + n_kv) // NORM_CHUNK, rope=True)
    k_c, v_c = _inproj1(ctx, csh1, csc1, w_in1[:, n_q:], gain[:, n_q:], ones, zeros, tm=n_ctx,
                        group_counts=(GQA_KV_HEADS, GQA_KV_HEADS), n_norm_chunks=n_kv // NORM_CHUNK, rope=False)
    y = _gqa_attention(q, k, v, k_c, v_c, tq=256)
    rw = jnp.pad(l1_router_w.astype(F32), ((0, 0), (0, LANES - N_EXPERTS)))
    x1, hm, route = _out1(x, y, g1, sh2, sc2, l1_w_out.astype(BF16), l1_ln1_g, l1_ln1_b, rw, tm=512)
    n_all = bsz * n_tok
    x1, hm, route = x1.reshape(n_all, D_MODEL), hm.reshape(n_all, D_MODEL), route.reshape(n_all, LANES)
    y_rows = _moe_experts(hm, _routing_plan(route, n_all), l1_moe_w13.astype(BF16), l1_moe_w2.astype(BF16))
    y_pairs = y_rows.reshape(-1, TOP_K * D_MODEL)
    out = _combine(x1, y_pairs, route, g2, l1_ln2_g, l1_ln2_b, tm=512, tiles_per_batch=n_tok // 512)
    return out.reshape(bsz, n_tok, D_MODEL)
```

```python
import functools
import math

import jax
import jax.numpy as jnp
import numpy as np
from jax import lax
from jax.experimental import pallas as pl
from jax.experimental.pallas import tpu as pltpu

F32 = jnp.float32
BF16 = jnp.bfloat16

D_MODEL = 1024
DEPTH = 2
GRID_W = 64
HEAD_DIM = 64
ROPE_THETA = 10000.0
NA_HEADS = 8
NA_WIN_H = 8
NA_WIN_W = 16
DIFF_HEADS = 4
DIFF_VDIM = 2 * HEAD_DIM
GQA_Q_HEADS = 16
GQA_KV_HEADS = 4
GQA_GROUP = GQA_Q_HEADS // GQA_KV_HEADS
FFN_DIM = 2816
N_EXPERTS = 8
TOP_K = 2
EXPERT_DIM = 3584
DEEPNORM_ALPHA = (2 * DEPTH) ** 0.25
LN_EPS = 1e-5
RMS_EPS = 1e-6
NEG_INF = -1e30
QK_SCALE = HEAD_DIM ** -0.5

LANES = 128
VMEM_LIMIT = 56 * 1024 * 1024

NA_TILE_ROWS = 4
NA_KEY_ROWS = 12
NA_TQ = NA_TILE_ROWS * GRID_W
NA_KEY_CHUNK = NA_TQ
NA_N_KEY_CHUNKS = NA_KEY_ROWS * GRID_W // NA_KEY_CHUNK

MOE_TM = 1024
MOE_TF = 1792
MOE_CHUNK = 256
MOE_NF = EXPERT_DIM // MOE_TF
MOE_ROWS_PER_STEP = MOE_TM // MOE_NF


def _cparams(sem, vmem=VMEM_LIMIT):
    return pltpu.CompilerParams(dimension_semantics=sem, vmem_limit_bytes=vmem)


def _dot(a, b):
    return jnp.dot(a, b, preferred_element_type=F32)


def _dot_t(a, b):
    return lax.dot_general(a, b, (((1,), (1,)), ((), ())), preferred_element_type=F32)


def _split(a):
    hi = a.astype(BF16)
    lo = (a - hi.astype(F32)).astype(BF16)
    return hi, lo


def _dot3(a, b):
    a_hi, a_lo = _split(a)
    b_hi, b_lo = _split(b)
    return _dot(a_hi, b_hi) + (_dot(a_hi, b_lo) + _dot(a_lo, b_hi))


def _dot2(a, b_exact):
    a_hi, a_lo = _split(a)
    return _dot(a_hi, b_exact) + _dot(a_lo, b_exact)


def _layer_norm(z, g, b):
    mu = jnp.mean(z, axis=-1, keepdims=True)
    zc = z - mu
    var = jnp.mean(zc * zc, axis=-1, keepdims=True)
    return zc * lax.rsqrt(var + LN_EPS) * g + b


def _silu(a):
    return a * jax.nn.sigmoid(a)


def _rope128(y, cos, sin_signed, lane_hi):
    swapped = jnp.where(lane_hi, pltpu.roll(y, 32, 1), pltpu.roll(y, 96, 1))
    return y * cos + swapped * sin_signed


def _store_groups(y, refs_iter):
    for g in range(y.shape[1] // HEAD_DIM):
        ref, idx = next(refs_iter)
        ref[0, idx] = y[:, g * HEAD_DIM:(g + 1) * HEAD_DIM].astype(BF16)


def _ada_kernel(c_ref, w_ref, b_ref, o_ref):
    o_ref[...] = _dot3(_silu(c_ref[...]), w_ref[...]) + b_ref[...]


def _ada_mods(cond, w, b):
    rows, tn = cond.shape[0], 1536
    n_out = w.shape[1]
    return pl.pallas_call(
        _ada_kernel,
        grid=(n_out // tn,),
        in_specs=[pl.BlockSpec((rows, D_MODEL), lambda j: (0, 0)),
                  pl.BlockSpec((D_MODEL, tn), lambda j: (0, j)),
                  pl.BlockSpec((1, tn), lambda j: (0, j))],
        out_specs=pl.BlockSpec((rows, tn), lambda j: (0, j)),
        out_shape=jax.ShapeDtypeStruct((rows, n_out), F32),
        compiler_params=_cparams(("arbitrary",)),
        name="ada_mods",
    )(cond, w, b.reshape(1, n_out))


def _inproj0_kernel(x_ref, sh_ref, sc_ref, w_ref, cos_ref, sin_ref,
                    qa_ref, ka_ref, va_ref, qb_ref, kb_ref, vb_ref, *, rope):
    h = (x_ref[0] * (1.0 + sc_ref[0]) + sh_ref[0]).astype(BF16)
    lane_hi = (lax.broadcasted_iota(jnp.int32, (1, LANES), 1) % HEAD_DIM) >= HEAD_DIM // 2
    sec = NA_HEADS * HEAD_DIM
    plan = ((qa_ref, QK_SCALE, False), (ka_ref, None, False), (va_ref, None, False),
            (qb_ref, QK_SCALE, True), (kb_ref, None, True))
    for s, (ref, scale, rotary) in enumerate(plan):
        p = _dot(h, w_ref[:, s * sec:(s + 1) * sec])
        if scale is not None:
            p = p * scale
        for c in range(sec // LANES):
            y = p[:, c * LANES:(c + 1) * LANES]
            if rotary and rope:
                y = _rope128(y, cos_ref[...], sin_ref[...], lane_hi)
            ref[0, 2 * c] = y[:, :HEAD_DIM].astype(BF16)
            ref[0, 2 * c + 1] = y[:, HEAD_DIM:].astype(BF16)
    p = _dot(h, w_ref[:, 5 * sec:6 * sec])
    for hh in range(DIFF_HEADS):
        vb_ref[0, hh] = p[:, hh * DIFF_VDIM:(hh + 1) * DIFF_VDIM].astype(BF16)


def _inproj0(x, shift, scale, w, cos, sin, *, tm, rope):
    bsz, n_tok, _ = x.shape
    per_batch = shift.shape[0] > 1
    mod_map = (lambda b, i: (b, 0, 0)) if per_batch else (lambda b, i: (0, 0, 0))
    hd = lambda n, d: jax.ShapeDtypeStruct((bsz, n, n_tok, d), BF16)
    hspec = lambda n, d: pl.BlockSpec((1, n, tm, d), lambda b, i: (b, 0, i, 0))
    return pl.pallas_call(
        functools.partial(_inproj0_kernel, rope=rope),
        grid=(bsz, n_tok // tm),
        in_specs=[pl.BlockSpec((1, tm, D_MODEL), lambda b, i: (b, i, 0)),
                  pl.BlockSpec((1, 1, D_MODEL), mod_map),
                  pl.BlockSpec((1, 1, D_MODEL), mod_map),
                  pl.BlockSpec(w.shape, lambda b, i: (0, 0)),
                  pl.BlockSpec((tm, LANES), lambda b, i: (i, 0)),
                  pl.BlockSpec((tm, LANES), lambda b, i: (i, 0))],
        out_specs=[hspec(8, 64), hspec(8, 64), hspec(8, 64), hspec(8, 64), hspec(8, 64), hspec(4, 128)],
        out_shape=[hd(8, 64), hd(8, 64), hd(8, 64), hd(8, 64), hd(8, 64), hd(4, 128)],
        compiler_params=_cparams(("arbitrary", "arbitrary")),
        name="inproj0",
    )(x, shift, scale, w, cos, sin)


def _scores(q, ks, biases):
    out = []
    for k, bias in zip(ks, biases):
        s = _dot_t(q, k)
        out.append(s if bias is None else s + bias)
    return out


def _softmax_parts(ss):
    m = ss[0].max(axis=-1, keepdims=True)
    for s in ss[1:]:
        m = jnp.maximum(m, s.max(axis=-1, keepdims=True))
    es = [jnp.exp(s - m) for s in ss]
    l = es[0].sum(axis=-1, keepdims=True)
    for e in es[1:]:
        l = l + e.sum(axis=-1, keepdims=True)
    return es, l


def _softmax_attend(q, ks, vs, biases):
    es, l = _softmax_parts(_scores(q, ks, biases))
    o = _dot(es[0].astype(BF16), vs[0])
    for e, v in zip(es[1:], vs[1:]):
        o = o + _dot(e.astype(BF16), v)
    return o * (1.0 / l)


def _diff_attend(q0, q1, k0s, k1s, vs, lam, subln_g, out_scale):
    e0, l0 = _softmax_parts(_scores(q0, k0s, [None] * len(k0s)))
    e1, l1 = _softmax_parts(_scores(q1, k1s, [None] * len(k1s)))
    r0 = 1.0 / l0
    r1 = lam / l1
    y = None
    for a0, a1, v in zip(e0, e1, vs):
        t = _dot((a0 * r0 - a1 * r1).astype(BF16), v)
        y = t if y is None else y + t
    y = y * lax.rsqrt(jnp.mean(y * y, axis=-1, keepdims=True) + RMS_EPS) * subln_g
    return y * out_scale


def _na_kernel(q_ref, k0_ref, k1_ref, k2_ref, v0_ref, v1_ref, v2_ref, kc_ref, vc_ref, bias_ref,
               o_ref, acc_ref):
    for h in range(NA_HEADS):
        ks = [k0_ref[0, h], k1_ref[0, h], k2_ref[0, h], kc_ref[0, h]]
        vs = [v0_ref[0, h], v1_ref[0, h], v2_ref[0, h], vc_ref[0, h]]
        biases = [bias_ref[0, h, :, j * NA_KEY_CHUNK:(j + 1) * NA_KEY_CHUNK] for j in range(NA_N_KEY_CHUNKS)]
        acc_ref[:, h * HEAD_DIM:(h + 1) * HEAD_DIM] = _softmax_attend(q_ref[0, h], ks, vs, biases + [None])
    o_ref[0] = acc_ref[...].astype(BF16)


def _na_bias_table(rpb):
    rows_total = 32
    n_h = rpb.shape[0]
    pad = GRID_W - NA_WIN_W
    rpb_pad = jnp.pad(rpb.astype(F32), ((0, 0), (0, 0), (pad, pad)))
    col_blocks = jnp.stack([rpb_pad[:, :, pad + NA_WIN_W - 1 - qc:pad + NA_WIN_W - 1 - qc + GRID_W]
                            for qc in range(GRID_W)], axis=2)
    q_c = np.arange(GRID_W)
    col_start = np.clip(q_c - NA_WIN_W // 2, 0, GRID_W - NA_WIN_W)
    col_ok = (q_c[None, :] >= col_start[:, None]) & (q_c[None, :] < col_start[:, None] + NA_WIN_W)
    col_blocks = jnp.where(jnp.asarray(col_ok)[None, None], col_blocks, NEG_INF)
    masked = jnp.full((n_h, GRID_W, GRID_W), NEG_INF, F32)
    kinds = []
    for r0 in (0, 4, 24, 28):
        ws = int(np.clip(r0 - NA_WIN_H // 2, 0, rows_total - NA_KEY_ROWS))
        q_rows = []
        for qr in range(r0, r0 + NA_TILE_ROWS):
            row_start = int(np.clip(qr - NA_WIN_H // 2, 0, rows_total - NA_WIN_H))
            blocks = []
            for kr in range(ws, ws + NA_KEY_ROWS):
                inside = row_start <= kr < row_start + NA_WIN_H
                blocks.append(col_blocks[:, kr - qr + NA_WIN_H - 1] if inside else masked)
            q_rows.append(jnp.concatenate(blocks, axis=-1))
        kinds.append(jnp.concatenate(q_rows, axis=1))
    return jnp.stack(kinds, axis=0)


def _na_attention(qa, ka, va, ka_c, va_c, bias_tab):
    bsz, n_h, n_tok, _ = qa.shape
    n_ctx = ka_c.shape[2]
    n_tiles = n_tok // NA_TQ
    last_start = (n_tok - NA_KEY_ROWS * GRID_W) // NA_KEY_CHUNK

    def win(j):
        return lambda t, b: (b, 0, jnp.clip(t - 1, 0, last_start) + j, 0)

    kv_spec = lambda j: pl.BlockSpec((1, n_h, NA_KEY_CHUNK, HEAD_DIM), win(j))
    ctx_spec = pl.BlockSpec((1, n_h, n_ctx, HEAD_DIM), lambda t, b: (b, 0, 0, 0))
    bias_map = lambda t, b: (jnp.minimum(t, 1) + jnp.maximum(t - (n_tiles - 3), 0), 0, 0, 0)
    return pl.pallas_call(
        _na_kernel,
        grid=(n_tiles, bsz),
        in_specs=[pl.BlockSpec((1, n_h, NA_TQ, HEAD_DIM), lambda t, b: (b, 0, t, 0)),
                  kv_spec(0), kv_spec(1), kv_spec(2), kv_spec(0), kv_spec(1), kv_spec(2),
                  ctx_spec, ctx_spec,
                  pl.BlockSpec((1, n_h, NA_TQ, NA_KEY_ROWS * GRID_W), bias_map)],
        out_specs=pl.BlockSpec((1, NA_TQ, n_h * HEAD_DIM), lambda t, b: (b, t, 0)),
        out_shape=jax.ShapeDtypeStruct((bsz, n_tok, n_h * HEAD_DIM), BF16),
        scratch_shapes=[pltpu.VMEM((NA_TQ, n_h * HEAD_DIM), F32)],
        compiler_params=_cparams(("arbitrary", "arbitrary")),
        name="na_attention",
    )(qa, ka, ka, ka, va, va, va, ka_c, va_c, bias_tab)


def _diff_kernel(lam_ref, q_ref, k_ref, kc_ref, v_ref, vc_ref, g_ref, o_ref, *, out_scale, with_latent):
    lam = lam_ref[0, 0]
    for h in range(DIFF_HEADS):
        if with_latent:
            k0s = [k_ref[0, 2 * h], kc_ref[0, 2 * h]]
            k1s = [k_ref[0, 2 * h + 1], kc_ref[0, 2 * h + 1]]
            vs = [v_ref[0, h], vc_ref[0, h]]
        else:
            k0s, k1s, vs = [kc_ref[0, 2 * h]], [kc_ref[0, 2 * h + 1]], [vc_ref[0, h]]
        y = _diff_attend(q_ref[0, 2 * h], q_ref[0, 2 * h + 1], k0s, k1s, vs, lam, g_ref[...], out_scale)
        o_ref[0, :, h * DIFF_VDIM:(h + 1) * DIFF_VDIM] = y.astype(BF16)


def _diff_attention(lam, qb, kb, vb, kb_c, vb_c, subln_g, *, tq, out_scale):
    bsz, _, n_tok, _ = qb.shape
    n_ctx = kb_c.shape[2]
    full = lambda a: pl.BlockSpec((1,) + a.shape[1:], lambda b, i: (b, 0, 0, 0))
    return pl.pallas_call(
        functools.partial(_diff_kernel, out_scale=out_scale, with_latent=True),
        grid=(bsz, n_tok // tq),
        in_specs=[pl.BlockSpec(memory_space=pltpu.SMEM),
                  pl.BlockSpec((1, 2 * DIFF_HEADS, tq, HEAD_DIM), lambda b, i: (b, 0, i, 0)),
                  full(kb), full(kb_c), full(vb), full(vb_c),
                  pl.BlockSpec((1, DIFF_VDIM), lambda b, i: (0, 0))],
        out_specs=pl.BlockSpec((1, tq, DIFF_HEADS * DIFF_VDIM), lambda b, i: (b, i, 0)),
        out_shape=jax.ShapeDtypeStruct((bsz, n_tok, DIFF_HEADS * DIFF_VDIM), BF16),
        compiler_params=_cparams(("arbitrary", "arbitrary")),
        name="diff_attention",
    )(lam, qb, kb, kb_c, vb, vb_c, subln_g.reshape(1, DIFF_VDIM))


def _ctx_attn0_kernel(lam_ref, qa_ref, ka_ref, va_ref, qb_ref, kb_ref, vb_ref, g_ref, ya_ref, yb_ref, acc_ref,
                      *, out_scale):
    for h in range(NA_HEADS):
        acc_ref[:, h * HEAD_DIM:(h + 1) * HEAD_DIM] = _softmax_attend(
            qa_ref[0, h], [ka_ref[0, h]], [va_ref[0, h]], [None])
    ya_ref[0] = acc_ref[...].astype(BF16)
    _diff_kernel(lam_ref, qb_ref, None, kb_ref, None, vb_ref, g_ref, yb_ref,
                 out_scale=out_scale, with_latent=False)


def _ctx_attention0(lam, qa, ka, va, qb, kb, vb, subln_g, *, out_scale):
    bsz, _, n_ctx, _ = qa.shape
    full = lambda a: pl.BlockSpec((1,) + a.shape[1:], lambda b: (b, 0, 0, 0))
    width = NA_HEADS * HEAD_DIM
    out = jax.ShapeDtypeStruct((bsz, n_ctx, width), BF16)
    return pl.pallas_call(
        functools.partial(_ctx_attn0_kernel, out_scale=out_scale),
        grid=(bsz,),
        in_specs=[pl.BlockSpec(memory_space=pltpu.SMEM),
                  full(qa), full(ka), full(va), full(qb), full(kb), full(vb),
                  pl.BlockSpec((1, DIFF_VDIM), lambda b: (0, 0))],
        out_specs=[pl.BlockSpec((1, n_ctx, width), lambda b: (b, 0, 0))] * 2,
        out_shape=[out, out],
        scratch_shapes=[pltpu.VMEM((n_ctx, width), F32)],
        compiler_params=_cparams(("arbitrary",)),
        name="ctx_attention0",
    )(lam, qa, ka, va, qb, kb, vb, subln_g.reshape(1, DIFF_VDIM))


FFN_CHUNK = FFN_DIM // 2


def _post0_kernel(x_ref, ya_ref, yb_ref, g1_ref, sh2_ref, sc2_ref, g2_ref, wo_ref, w13_ref, w2_ref,
                  ln1g_ref, ln1b_ref, ln2g_ref, ln2b_ref, o_ref):
    half = wo_ref.shape[0] // 2
    y = _dot(ya_ref[0], wo_ref[:half, :]) + _dot(yb_ref[0], wo_ref[half:, :])
    x1 = _layer_norm(DEEPNORM_ALPHA * x_ref[0] + g1_ref[0] * y, ln1g_ref[...], ln1b_ref[...])
    h = (x1 * (1.0 + sc2_ref[0]) + sh2_ref[0]).astype(BF16)
    f = None
    for c in range(FFN_DIM // FFN_CHUNK):
        a = _dot(h, w13_ref[:, c * FFN_CHUNK:(c + 1) * FFN_CHUNK])
        g = _dot(h, w13_ref[:, FFN_DIM + c * FFN_CHUNK:FFN_DIM + (c + 1) * FFN_CHUNK])
        t = _dot((_silu(a) * g).astype(BF16), w2_ref[c * FFN_CHUNK:(c + 1) * FFN_CHUNK, :])
        f = t if f is None else f + t
    o_ref[0] = _layer_norm(DEEPNORM_ALPHA * x1 + g2_ref[0] * f, ln2g_ref[...], ln2b_ref[...])


def _post0(x, ya, yb, g1, sh2, sc2, g2, wo, w13, w2, ln1g, ln1b, ln2g, ln2b, *, tm):
    bsz, n_tok, _ = x.shape
    per_batch = g1.shape[0] > 1
    mod_map = (lambda b, i: (b, 0, 0)) if per_batch else (lambda b, i: (0, 0, 0))
    mod = pl.BlockSpec((1, 1, D_MODEL), mod_map)
    const = lambda a: pl.BlockSpec(a.shape, lambda b, i: (0,) * a.ndim, pipeline_mode=pl.Buffered(1))
    vec = lambda a: a.reshape(1, D_MODEL)
    half = ya.shape[2]
    return pl.pallas_call(
        _post0_kernel,
        grid=(bsz, n_tok // tm),
        in_specs=[pl.BlockSpec((1, tm, D_MODEL), lambda b, i: (b, i, 0)),
                  pl.BlockSpec((1, tm, half), lambda b, i: (b, i, 0)),
                  pl.BlockSpec((1, tm, half), lambda b, i: (b, i, 0)),
                  mod, mod, mod, mod, const(wo), const(w13), const(w2),
                  const(vec(ln1g)), const(vec(ln1b)), const(vec(ln2g)), const(vec(ln2b))],
        out_specs=pl.BlockSpec((1, tm, D_MODEL), lambda b, i: (b, i, 0)),
        out_shape=jax.ShapeDtypeStruct(x.shape, F32),
        compiler_params=_cparams(("arbitrary", "arbitrary")),
        name="post0",
    )(x, ya, yb, g1, sh2, sc2, g2, wo, w13, w2, vec(ln1g), vec(ln1b), vec(ln2g), vec(ln2b))


NORM_CHUNK = 4 * HEAD_DIM


def _inproj1_kernel(x_ref, sh_ref, sc_ref, w_ref, gain_ref, ind_ref, indt_ref, cos_ref, sin_ref, *out_refs,
                    n_norm_chunks, group_counts, rope):
    h = (x_ref[0] * (1.0 + sc_ref[0]) + sh_ref[0]).astype(BF16)
    lane_hi = (lax.broadcasted_iota(jnp.int32, (1, LANES), 1) % HEAD_DIM) >= HEAD_DIM // 2
    targets = iter([(ref, i) for ref, n in zip(out_refs, group_counts) for i in range(n)])
    n_chunks = w_ref.shape[1] // NORM_CHUNK
    for c in range(n_chunks):
        cols = slice(c * NORM_CHUNK, (c + 1) * NORM_CHUNK)
        p = _dot(h, w_ref[:, cols])
        if c < n_norm_chunks:
            ms = _dot2(p * p, ind_ref[...]) * (1.0 / HEAD_DIM)
            r = _dot2(lax.rsqrt(ms + RMS_EPS), indt_ref[...])
            p = p * r * gain_ref[:, cols]
            if rope:
                p = jnp.concatenate(
                    [_rope128(p[:, j * LANES:(j + 1) * LANES], cos_ref[...], sin_ref[...], lane_hi)
                     for j in range(NORM_CHUNK // LANES)], axis=1)
        _store_groups(p, targets)


def _inproj1(x, shift, scale, w, gain, cos, sin, *, tm, group_counts, n_norm_chunks, rope):
    bsz, n_tok, _ = x.shape
    per_batch = shift.shape[0] > 1
    mod_map = (lambda b, i: (b, 0, 0)) if per_batch else (lambda b, i: (0, 0, 0))
    ind_np = np.zeros((NORM_CHUNK, LANES), np.float32)
    ind_np[np.arange(NORM_CHUNK), np.arange(NORM_CHUNK) // HEAD_DIM] = 1.0
    ind = jnp.asarray(ind_np, BF16)
    indt = jnp.asarray(ind_np.T, BF16)
    const = lambda a: pl.BlockSpec(a.shape, lambda b, i: (0,) * a.ndim)
    return pl.pallas_call(
        functools.partial(_inproj1_kernel, n_norm_chunks=n_norm_chunks, group_counts=group_counts, rope=rope),
        grid=(bsz, n_tok // tm),
        in_specs=[pl.BlockSpec((1, tm, D_MODEL), lambda b, i: (b, i, 0)),
                  pl.BlockSpec((1, 1, D_MODEL), mod_map),
                  pl.BlockSpec((1, 1, D_MODEL), mod_map),
                  const(w), const(gain), const(ind), const(indt),
                  pl.BlockSpec((tm, LANES), lambda b, i: (i, 0)),
                  pl.BlockSpec((tm, LANES), lambda b, i: (i, 0))],
        out_specs=[pl.BlockSpec((1, n, tm, HEAD_DIM), lambda b, i: (b, 0, i, 0)) for n in group_counts],
        out_shape=[jax.ShapeDtypeStruct((bsz, n, n_tok, HEAD_DIM), BF16) for n in group_counts],
        compiler_params=_cparams(("arbitrary", "arbitrary")),
        name="inproj1",
    )(x, shift, scale, w, gain, ind, indt, cos, sin)


def _gqa_kernel(q_ref, k_ref, kc_ref, v_ref, vc_ref, o_ref, acc_ref):
    for h in range(GQA_Q_HEADS):
        g = h // GQA_GROUP
        acc_ref[:, h * HEAD_DIM:(h + 1) * HEAD_DIM] = _softmax_attend(
            q_ref[0, h], [k_ref[0, g], kc_ref[0, g]], [v_ref[0, g], vc_ref[0, g]], [None, None])
    o_ref[0] = acc_ref[...].astype(BF16)


def _gqa_attention(q, k, v, k_c, v_c, *, tq):
    bsz, n_h, n_tok, _ = q.shape
    full = lambda a: pl.BlockSpec((1,) + a.shape[1:], lambda b, i: (b, 0, 0, 0))
    return pl.pallas_call(
        _gqa_kernel,
        grid=(bsz, n_tok // tq),
        in_specs=[pl.BlockSpec((1, n_h, tq, HEAD_DIM), lambda b, i: (b, 0, i, 0)),
                  full(k), full(k_c), full(v), full(v_c)],
        out_specs=pl.BlockSpec((1, tq, n_h * HEAD_DIM), lambda b, i: (b, i, 0)),
        out_shape=jax.ShapeDtypeStruct((bsz, n_tok, n_h * HEAD_DIM), BF16),
        scratch_shapes=[pltpu.VMEM((tq, n_h * HEAD_DIM), F32)],
        compiler_params=_cparams(("arbitrary", "arbitrary")),
        name="gqa_attention",
    )(q, k, k_c, v, v_c)


def _out1_kernel(x_ref, y_ref, g1_ref, sh2_ref, sc2_ref, wo_ref, lng_ref, lnb_ref, rw_ref,
                 x1_ref, hm_ref, route_ref):
    y = _dot(y_ref[0], wo_ref[...])
    x1 = _layer_norm(DEEPNORM_ALPHA * x_ref[0] + g1_ref[0] * y, lng_ref[...], lnb_ref[...])
    x1_ref[0] = x1
    hm = x1 * (1.0 + sc2_ref[0]) + sh2_ref[0]
    hm_ref[0] = hm
    lane = lax.broadcasted_iota(jnp.int32, (hm.shape[0], LANES), 1)
    logits = jnp.where(lane < N_EXPERTS, _dot3(hm, rw_ref[...]), NEG_INF)
    v1 = logits.max(axis=-1, keepdims=True)
    i1 = jnp.where(logits == v1, lane, LANES).min(axis=-1, keepdims=True)
    rest = jnp.where(lane == i1, NEG_INF, logits)
    v2 = rest.max(axis=-1, keepdims=True)
    i2 = jnp.where(rest == v2, lane, LANES).min(axis=-1, keepdims=True)
    e = jnp.exp(v2 - v1)
    inv = 1.0 / (1.0 + e)
    route_ref[0] = jnp.where(lane == 0, i1.astype(F32),
                             jnp.where(lane == 1, i2.astype(F32),
                                       jnp.where(lane == 2, inv, jnp.where(lane == 3, e * inv, 0.0))))


def _out1(x, y, g1, sh2, sc2, wo, lng, lnb, rw, *, tm):
    bsz, n_tok, _ = x.shape
    mod = pl.BlockSpec((1, 1, D_MODEL), lambda b, i: (b, 0, 0))
    tile = pl.BlockSpec((1, tm, D_MODEL), lambda b, i: (b, i, 0))
    const = lambda a: pl.BlockSpec(a.shape, lambda b, i: (0,) * a.ndim)
    vec = lambda a: a.reshape(1, D_MODEL)
    return pl.pallas_call(
        _out1_kernel,
        grid=(bsz, n_tok // tm),
        in_specs=[tile, tile, mod, mod, mod, const(wo), const(vec(lng)), const(vec(lnb)), const(rw)],
        out_specs=[tile, tile, pl.BlockSpec((1, tm, LANES), lambda b, i: (b, i, 0))],
        out_shape=[jax.ShapeDtypeStruct(x.shape, F32), jax.ShapeDtypeStruct(x.shape, F32),
                   jax.ShapeDtypeStruct((bsz, n_tok, LANES), F32)],
        compiler_params=_cparams(("arbitrary", "arbitrary")),
        name="out1",
    )(x, y, g1, sh2, sc2, wo, vec(lng), vec(lnb), rw)


def _moe_kernel(be_ref, nv_ref, gfirst_ref, gnext_ref, sprev_ref, slast_ref, hm_ref, w1_ref, w3_ref, w2_ref,
                y_ref, xg_ref, xb_ref, hid_ref, acc_ref, gsem, ssem):
    del be_ref
    i, j = pl.program_id(0), pl.program_id(1)
    n_valid = nv_ref[0]
    slot = i % 2
    other = 1 - slot

    def gather_row(idx_ref, row, dst_slot):
        return pltpu.make_async_copy(hm_ref.at[pl.ds(idx_ref[0, 0, row], 1)],
                                     xg_ref.at[dst_slot, pl.ds(row, 1)], gsem.at[dst_slot])

    def scatter_row(idx_ref, row, src_slot):
        return pltpu.make_async_copy(acc_ref.at[src_slot, pl.ds(row, 1)],
                                     y_ref.at[pl.ds(idx_ref[0, 0, row], 1)], ssem.at[src_slot])

    def wait_gather(s):
        pltpu.make_async_copy(hm_ref.at[pl.ds(0, MOE_TM)], xg_ref.at[s], gsem.at[s]).wait()

    def wait_scatter(s):
        pltpu.make_async_copy(acc_ref.at[s], y_ref.at[pl.ds(0, MOE_TM)], ssem.at[s]).wait()

    @pl.when(jnp.logical_and(i == 0, j == 0))
    def _():
        acc_ref[1] = jnp.zeros((MOE_TM, D_MODEL), F32)

        def start(r, carry):
            gather_row(gfirst_ref, r, 0).start()
            return carry

        lax.fori_loop(0, MOE_TM, start, 0)
        n_real = y_ref.shape[0] - 2 * MOE_TM
        fill = pltpu.make_async_copy(acc_ref.at[1], y_ref.at[pl.ds(n_real, MOE_TM)], ssem.at[1])
        fill.start()
        fill.wait()

    @pl.when(jnp.logical_and(i < n_valid, j == 0))
    def _():
        wait_gather(slot)
        xb_ref[...] = xg_ref[slot].astype(BF16)

        @pl.when(i >= 1)
        def _():
            wait_scatter(slot)

        acc_ref[slot] = jnp.zeros((MOE_TM, D_MODEL), F32)

    @pl.when(i < n_valid)
    def _():
        base = j * MOE_ROWS_PER_STEP
        n_up = MOE_TF // MOE_CHUNK
        n_down = D_MODEL // MOE_CHUNK
        per_group = -(-MOE_ROWS_PER_STEP // (n_up + n_down))
        rows = iter(range(MOE_ROWS_PER_STEP))

        def copy_group():
            for _ in range(per_group):
                r = next(rows, None)
                if r is not None:
                    gather_row(gnext_ref, base + r, other).start(priority=r % 2)
                    scatter_row(sprev_ref, base + r, other).start(priority=r % 2)

        cols = lambda n: slice(n * MOE_CHUNK, (n + 1) * MOE_CHUNK)
        up = lambda n: (_dot(xb_ref[...], w1_ref[0, :, cols(n)]), _dot(xb_ref[...], w3_ref[0, :, cols(n)]))
        pending = up(0)
        for n in range(n_up):
            following = up(n + 1) if n + 1 < n_up else None
            a, g = pending
            hid_ref[:, cols(n)] = (_silu(a) * g).astype(BF16)
            copy_group()
            pending = following
        down = lambda n: _dot(hid_ref[...], w2_ref[0, :, cols(n)])
        pending = down(0)
        for n in range(n_down):
            following = down(n + 1) if n + 1 < n_down else None
            acc_ref[slot, :, cols(n)] += pending
            copy_group()
            pending = following

    @pl.when(jnp.logical_and(i == n_valid - 1, j == MOE_NF - 1))
    def _():
        def start(r, carry):
            scatter_row(slast_ref, r, slot).start()
            return carry

        lax.fori_loop(0, MOE_TM, start, 0)
        wait_gather(other)
        wait_scatter(other)
        wait_scatter(slot)


def _moe_experts(hm_packed, plan, w13, w2):
    block_e, n_valid, gidx, sidx, n_out_rows = plan
    n_blocks = block_e.shape[0]
    smem_rows = lambda index_map: pl.BlockSpec((1, 1, MOE_TM), index_map, memory_space=pltpu.SMEM)
    return pl.pallas_call(
        _moe_kernel,
        grid_spec=pltpu.PrefetchScalarGridSpec(
            num_scalar_prefetch=2,
            grid=(n_blocks, MOE_NF),
            in_specs=[smem_rows(lambda i, j, be, nv: (0, 0, 0)),
                      smem_rows(lambda i, j, be, nv: (i + 1, 0, 0)),
                      smem_rows(lambda i, j, be, nv: (i, 0, 0)),
                      smem_rows(lambda i, j, be, nv: (nv[0], 0, 0)),
                      pl.BlockSpec(memory_space=pl.ANY),
                      pl.BlockSpec((1, D_MODEL, MOE_TF), lambda i, j, be, nv: (be[i], 0, j)),
                      pl.BlockSpec((1, D_MODEL, MOE_TF), lambda i, j, be, nv: (be[i], 0, MOE_NF + j)),
                      pl.BlockSpec((1, MOE_TF, D_MODEL), lambda i, j, be, nv: (be[i], j, 0))],
            out_specs=pl.BlockSpec(memory_space=pl.ANY),
            scratch_shapes=[pltpu.VMEM((2, MOE_TM, D_MODEL), F32),
                            pltpu.VMEM((MOE_TM, D_MODEL), BF16),
                            pltpu.VMEM((MOE_TM, MOE_TF), BF16),
                            pltpu.VMEM((2, MOE_TM, D_MODEL), F32),
                            pltpu.SemaphoreType.DMA((2,)),
                            pltpu.SemaphoreType.DMA((2,))]),
        out_shape=jax.ShapeDtypeStruct((n_out_rows, D_MODEL), F32),
        compiler_params=_cparams(("arbitrary", "arbitrary")),
        name="moe_experts",
    )(block_e, n_valid, gidx, gidx, sidx, sidx, hm_packed, w13, w13, w2)


def _combine_kernel(x1_ref, y0_ref, y1_ref, route_ref, g2_ref, lng_ref, lnb_ref, o_ref):
    route = route_ref[...]
    m = y0_ref[...] * route[:, 2:3] + y1_ref[...] * route[:, 3:4]
    o_ref[...] = _layer_norm(DEEPNORM_ALPHA * x1_ref[...] + g2_ref[0] * m, lng_ref[...], lnb_ref[...])


def _combine(x1, y_rows, route, g2, lng, lnb, *, tm, tiles_per_batch):
    n_tok = x1.shape[0]
    vec = lambda a: a.reshape(1, D_MODEL)
    const = lambda a: pl.BlockSpec(a.shape, lambda i: (0,) * a.ndim)
    return pl.pallas_call(
        _combine_kernel,
        grid=(n_tok // tm,),
        in_specs=[pl.BlockSpec((tm, D_MODEL), lambda i: (i, 0)),
                  pl.BlockSpec((tm, D_MODEL), lambda i: (i, 0)),
                  pl.BlockSpec((tm, D_MODEL), lambda i: (i + n_tok // tm, 0)),
                  pl.BlockSpec((tm, LANES), lambda i: (i, 0)),
                  pl.BlockSpec((1, 1, D_MODEL), lambda i: (i // tiles_per_batch, 0, 0)),
                  const(vec(lng)), const(vec(lnb))],
        out_specs=pl.BlockSpec((tm, D_MODEL), lambda i: (i, 0)),
        out_shape=jax.ShapeDtypeStruct(x1.shape, F32),
        compiler_params=_cparams(("arbitrary",)),
        name="moe_combine",
    )(x1, y_rows, y_rows, route, g2, vec(lng), vec(lnb))


def _routing_plan(route, n_tok):
    n_pairs = n_tok * TOP_K
    flat_e = route[:, :TOP_K].astype(jnp.int32).reshape(-1)
    pair = jnp.arange(n_pairs, dtype=jnp.int32)
    sorted_key = jnp.sort(flat_e * n_pairs + pair)
    order = sorted_key - (sorted_key // n_pairs) * n_pairs
    experts = jnp.arange(N_EXPERTS, dtype=jnp.int32)
    counts = jnp.sum((flat_e[:, None] == experts[None, :]).astype(jnp.int32), axis=0)
    first_pair = jnp.cumsum(counts) - counts
    blocks_per = (counts + MOE_TM - 1) // MOE_TM
    block_end = jnp.cumsum(blocks_per)
    block_start = block_end - blocks_per
    n_blocks = -(-(n_pairs + N_EXPERTS * (MOE_TM - 1)) // MOE_TM)
    blk = jnp.arange(n_blocks, dtype=jnp.int32)
    block_e = jnp.minimum(jnp.sum((block_end[None, :] <= blk[:, None]).astype(jnp.int32), axis=1), N_EXPERTS - 1)
    n_valid = block_end[-1]
    sel = (block_e[:, None] == experts[None, :]).astype(jnp.int32)
    pick = lambda v: jnp.sum(sel * v[None, :], axis=1)
    offset = (blk - pick(block_start)) * MOE_TM
    src = jnp.clip(pick(first_pair) + offset, 0, n_pairs)
    r = jnp.arange(MOE_TM, dtype=jnp.int32)[None, :]
    pairs = jnp.take(order, jnp.minimum(src[:, None] + r, n_pairs - 1), axis=0)
    used = jnp.logical_and((blk < n_valid)[:, None], offset[:, None] + r < pick(counts)[:, None])
    spare = n_pairs + (blk % 2)[:, None] * MOE_TM + r
    token, choice = pairs // TOP_K, pairs % TOP_K
    gidx = jnp.where(used, token, 0)
    sidx = jnp.where(used, choice * n_tok + token, spare)
    gidx = jnp.concatenate([gidx, jnp.zeros((1, MOE_TM), jnp.int32)], axis=0)
    sidx = jnp.concatenate([n_pairs + MOE_TM + r, sidx], axis=0)
    shape3 = (n_blocks + 1, 1, MOE_TM)
    return (block_e.astype(jnp.int32), n_valid.astype(jnp.int32).reshape(1), gidx.reshape(shape3),
            sidx.reshape(shape3), n_pairs + 2 * MOE_TM)


def _rope_tables(n_tok, n_identity=0):
    t = jnp.arange(n_tok, dtype=jnp.int32)
    n_freq = HEAD_DIM // 4
    inv_freq = ROPE_THETA ** (-jnp.arange(n_freq, dtype=F32) / n_freq)
    ang = jnp.concatenate([(t // GRID_W).astype(F32)[:, None] * inv_freq,
                           (t % GRID_W).astype(F32)[:, None] * inv_freq], axis=-1)
    cos, sin = jnp.cos(ang), jnp.sin(ang)
    cos128 = jnp.tile(cos, (1, 4))
    sin128 = jnp.tile(jnp.concatenate([-sin, sin], axis=-1), (1, 2))
    return cos128, sin128


def _mods(cond_rows, ada_w, ada_b, bsz):
    out = _ada_mods(cond_rows, ada_w, ada_b)
    lat = [m.reshape(bsz, 1, D_MODEL) for m in jnp.split(out[:bsz], 6, axis=-1)]
    ctx = [m.reshape(1, 1, D_MODEL) for m in jnp.split(out[bsz:bsz + 1], 6, axis=-1)]
    return lat, ctx


def kernel(x, c, ctx, c_ctx, l0_ada_w, l0_ada_b, l0_w_in, l0_rpb, l0_lambda_qk, l0_subln_g, l0_w_out, l0_ln1_g, l0_ln1_b, l0_ffn_w13, l0_ffn_w2, l0_ln2_g, l0_ln2_b, l1_ada_w, l1_ada_b, l1_w_in, l1_q_norm_g, l1_k_norm_g, l1_w_out, l1_ln1_g, l1_ln1_b, l1_router_w, l1_moe_w13, l1_moe_w2, l1_ln2_g, l1_ln2_b):
    bsz, n_tok, _ = x.shape
    n_ctx = ctx.shape[1]
    cond_rows = jnp.concatenate([c, c_ctx[None, :], jnp.zeros((32 - bsz - 1, D_MODEL), F32)], axis=0)
    cos, sin = _rope_tables(n_tok)
    ones = jnp.ones((n_ctx, LANES), F32)
    zeros = jnp.zeros((n_ctx, LANES), F32)

    (sh1, sc1, g1, sh2, sc2, g2), (csh1, csc1, cg1, csh2, csc2, cg2) = _mods(cond_rows, l0_ada_w, l0_ada_b, bsz)
    w_in0 = l0_w_in.astype(BF16)
    qa, ka, va, qb, kb, vb = _inproj0(x, sh1, sc1, w_in0, cos, sin, tm=512, rope=True)
    qa_c, ka_c, va_c, qb_c, kb_c, vb_c = _inproj0(ctx, csh1, csc1, w_in0, ones, zeros, tm=n_ctx, rope=False)
    lam_init = 0.8 - 0.6 * math.exp(-0.3 * 0)
    lq = l0_lambda_qk.astype(F32)
    lam = (jnp.exp(jnp.sum(lq[0] * lq[1])) - jnp.exp(jnp.sum(lq[2] * lq[3])) + lam_init).reshape(1, 1)
    y_a = _na_attention(qa, ka, va, ka_c, va_c, _na_bias_table(l0_rpb))
    y_b = _diff_attention(lam, qb, kb, vb, kb_c, vb_c, l0_subln_g, tq=256, out_scale=1.0 - lam_init)
    ya_c, yb_c = _ctx_attention0(lam, qa_c, ka_c, va_c, qb_c, kb_c, vb_c, l0_subln_g, out_scale=1.0 - lam_init)
    wo0, w13_0, w2_0 = l0_w_out.astype(BF16), l0_ffn_w13.astype(BF16), l0_ffn_w2.astype(BF16)
    ln0 = (l0_ln1_g, l0_ln1_b, l0_ln2_g, l0_ln2_b)
    x = _post0(x, y_a, y_b, g1, sh2, sc2, g2, wo0, w13_0, w2_0, *ln0, tm=512)
    ctx = _post0(ctx, ya_c, yb_c, cg1, csh2, csc2, cg2, wo0, w13_0, w2_0, *ln0, tm=n_ctx)

    (sh1, sc1, g1, sh2, sc2, g2), (csh1, csc1, _, _, _, _) = _mods(cond_rows, l1_ada_w, l1_ada_b, bsz)
    w_in1 = l1_w_in.astype(BF16)
    n_q, n_kv = GQA_Q_HEADS * HEAD_DIM, GQA_KV_HEADS * HEAD_DIM
    gain = jnp.concatenate([jnp.tile(l1_q_norm_g, GQA_Q_HEADS) * QK_SCALE,
                            jnp.tile(l1_k_norm_g, GQA_KV_HEADS)]).reshape(1, n_q + n_kv)
    q, k, v = _inproj1(x, sh1, sc1, w_in1, gain, cos, sin, tm=512,
                       group_counts=(GQA_Q_HEADS, GQA_KV_HEADS, GQA_KV_HEADS),
                       n_norm_chunks=(n_q + n_kv) // NORM_CHUNK, rope=True)
    k_c, v_c = _inproj1(ctx, csh1, csc1, w_in1[:, n_q:], gain[:, n_q:], ones, zeros, tm=n_ctx,
                        group_counts=(GQA_KV_HEADS, GQA_KV_HEADS), n_norm_chunks=n_kv // NORM_CHUNK, rope=False)
    y = _gqa_attention(q, k, v, k_c, v_c, tq=256)
    rw = jnp.pad(l1_router_w.astype(F32), ((0, 0), (0, LANES - N_EXPERTS)))
    x1, hm, route = _out1(x, y, g1, sh2, sc2, l1_w_out.astype(BF16), l1_ln1_g, l1_ln1_b, rw, tm=512)
    n_all = bsz * n_tok
    x1, hm, route = x1.reshape(n_all, D_MODEL), hm.reshape(n_all, D_MODEL), route.reshape(n_all, LANES)
    y_rows = _moe_experts(hm, _routing_plan(route, n_all), l1_moe_w13.astype(BF16), l1_moe_w2.astype(BF16))
    out = _combine(x1, y_rows, route, g2, l1_ln2_g, l1_ln2_b, tm=512, tiles_per_batch=n_tok // 512)
    return out.reshape(bsz, n_tok, D_MODEL)
```

```python
import functools
import math

import jax
import jax.numpy as jnp
import numpy as np
from jax import lax
from jax.experimental import pallas as pl
from jax.experimental.pallas import tpu as pltpu

F32 = jnp.float32
BF16 = jnp.bfloat16

D_MODEL = 1024
DEPTH = 2
GRID_W = 64
HEAD_DIM = 64
ROPE_THETA = 10000.0
NA_HEADS = 8
NA_WIN_H = 8
NA_WIN_W = 16
DIFF_HEADS = 4
DIFF_VDIM = 2 * HEAD_DIM
GQA_Q_HEADS = 16
GQA_KV_HEADS = 4
GQA_GROUP = GQA_Q_HEADS // GQA_KV_HEADS
FFN_DIM = 2816
N_EXPERTS = 8
TOP_K = 2
EXPERT_DIM = 3584
DEEPNORM_ALPHA = (2 * DEPTH) ** 0.25
LN_EPS = 1e-5
RMS_EPS = 1e-6
NEG_INF = -1e30
QK_SCALE = HEAD_DIM ** -0.5

LANES = 128
VMEM_LIMIT = 56 * 1024 * 1024

NA_TILE_ROWS = 4
NA_KEY_ROWS = 12
NA_TQ = NA_TILE_ROWS * GRID_W
NA_KEY_CHUNK = NA_TQ
NA_N_KEY_CHUNKS = NA_KEY_ROWS * GRID_W // NA_KEY_CHUNK

MOE_TM = 1024
MOE_TF = 1792
MOE_CHUNK = 256
MOE_NF = EXPERT_DIM // MOE_TF
MOE_ROWS_PER_STEP = MOE_TM // MOE_NF


def _cparams(sem, vmem=VMEM_LIMIT):
    return pltpu.CompilerParams(dimension_semantics=sem, vmem_limit_bytes=vmem)


def _dot(a, b):
    return jnp.dot(a, b, preferred_element_type=F32)


def _dot_t(a, b):
    return lax.dot_general(a, b, (((1,), (1,)), ((), ())), preferred_element_type=F32)


def _split(a):
    hi = a.astype(BF16)
    lo = (a - hi.astype(F32)).astype(BF16)
    return hi, lo


def _dot3(a, b):
    a_hi, a_lo = _split(a)
    b_hi, b_lo = _split(b)
    return _dot(a_hi, b_hi) + (_dot(a_hi, b_lo) + _dot(a_lo, b_hi))


def _dot2(a, b_exact):
    a_hi, a_lo = _split(a)
    return _dot(a_hi, b_exact) + _dot(a_lo, b_exact)


def _layer_norm(z, g, b):
    mu = jnp.mean(z, axis=-1, keepdims=True)
    zc = z - mu
    var = jnp.mean(zc * zc, axis=-1, keepdims=True)
    return zc * lax.rsqrt(var + LN_EPS) * g + b


def _silu(a):
    return a * jax.nn.sigmoid(a)


def _rope128(y, cos, sin_signed, lane_hi):
    swapped = jnp.where(lane_hi, pltpu.roll(y, 32, 1), pltpu.roll(y, 96, 1))
    return y * cos + swapped * sin_signed


def _head_halves(pair):
    lane = lax.broadcasted_iota(jnp.int32, pair.shape, 1)
    zero = jnp.zeros_like(pair)
    return jnp.where(lane < HEAD_DIM, pair, zero), jnp.where(lane >= HEAD_DIM, pair, zero)


def _merge_halves(lo, hi):
    lane = lax.broadcasted_iota(jnp.int32, lo.shape, 1)
    return jnp.where(lane < HEAD_DIM, lo, hi)


def _store_pairs(y, refs_iter):
    lane = lax.broadcasted_iota(jnp.int32, (1, LANES), 1)
    for c in range(y.shape[1] // LANES):
        slab = y[:, c * LANES:(c + 1) * LANES]
        ref, idx, dup = next(refs_iter)
        if dup:
            rolled = pltpu.roll(slab, HEAD_DIM, 1)
            ref[0, idx] = jnp.where(lane < HEAD_DIM, slab, rolled).astype(BF16)
            ref[0, idx + 1] = jnp.where(lane < HEAD_DIM, rolled, slab).astype(BF16)
        else:
            ref[0, idx] = slab.astype(BF16)


def _ada_kernel(c_ref, w_ref, b_ref, o_ref):
    o_ref[...] = _dot3(_silu(c_ref[...]), w_ref[...]) + b_ref[...]


def _ada_mods(cond, w, b):
    rows, tn = cond.shape[0], 1536
    n_out = w.shape[1]
    return pl.pallas_call(
        _ada_kernel,
        grid=(n_out // tn,),
        in_specs=[pl.BlockSpec((rows, D_MODEL), lambda j: (0, 0)),
                  pl.BlockSpec((D_MODEL, tn), lambda j: (0, j)),
                  pl.BlockSpec((1, tn), lambda j: (0, j))],
        out_specs=pl.BlockSpec((rows, tn), lambda j: (0, j)),
        out_shape=jax.ShapeDtypeStruct((rows, n_out), F32),
        compiler_params=_cparams(("arbitrary",)),
        name="ada_mods",
    )(cond, w, b.reshape(1, n_out))


def _inproj0_kernel(x_ref, sh_ref, sc_ref, w_ref, cos_ref, sin_ref,
                    qa_ref, ka_ref, va_ref, qb_ref, kb_ref, vb_ref, *, rope):
    h = (x_ref[0] * (1.0 + sc_ref[0]) + sh_ref[0]).astype(BF16)
    lane_hi = (lax.broadcasted_iota(jnp.int32, (1, LANES), 1) % HEAD_DIM) >= HEAD_DIM // 2
    sec = NA_HEADS * HEAD_DIM
    plan = ((qa_ref, QK_SCALE, False), (ka_ref, None, False), (va_ref, None, False),
            (qb_ref, QK_SCALE, True), (kb_ref, None, True))
    for s, (ref, scale, rotary) in enumerate(plan):
        p = _dot(h, w_ref[:, s * sec:(s + 1) * sec])
        if scale is not None:
            p = p * scale
        for c in range(sec // LANES):
            y = p[:, c * LANES:(c + 1) * LANES]
            if rotary and rope:
                y = _rope128(y, cos_ref[...], sin_ref[...], lane_hi)
            ref[0, c] = y.astype(BF16)
    p = _dot(h, w_ref[:, 5 * sec:6 * sec])
    for hh in range(DIFF_HEADS):
        vb_ref[0, hh] = p[:, hh * DIFF_VDIM:(hh + 1) * DIFF_VDIM].astype(BF16)


def _inproj0(x, shift, scale, w, cos, sin, *, tm, rope):
    bsz, n_tok, _ = x.shape
    per_batch = shift.shape[0] > 1
    mod_map = (lambda b, i: (b, 0, 0)) if per_batch else (lambda b, i: (0, 0, 0))
    n_pairs = NA_HEADS // 2
    out_shape = jax.ShapeDtypeStruct((bsz, n_pairs, n_tok, LANES), BF16)
    out_spec = pl.BlockSpec((1, n_pairs, tm, LANES), lambda b, i: (b, 0, i, 0))
    return pl.pallas_call(
        functools.partial(_inproj0_kernel, rope=rope),
        grid=(bsz, n_tok // tm),
        in_specs=[pl.BlockSpec((1, tm, D_MODEL), lambda b, i: (b, i, 0)),
                  pl.BlockSpec((1, 1, D_MODEL), mod_map),
                  pl.BlockSpec((1, 1, D_MODEL), mod_map),
                  pl.BlockSpec(w.shape, lambda b, i: (0, 0)),
                  pl.BlockSpec((tm, LANES), lambda b, i: (i, 0)),
                  pl.BlockSpec((tm, LANES), lambda b, i: (i, 0))],
        out_specs=[out_spec] * 6,
        out_shape=[out_shape] * 6,
        compiler_params=_cparams(("arbitrary", "arbitrary")),
        name="inproj0",
    )(x, shift, scale, w, cos, sin)


def _scores(q, ks, biases):
    out = []
    for k, bias in zip(ks, biases):
        s = _dot_t(q, k)
        out.append(s if bias is None else s + bias)
    return out


def _softmax_parts(ss):
    m = ss[0].max(axis=-1, keepdims=True)
    for s in ss[1:]:
        m = jnp.maximum(m, s.max(axis=-1, keepdims=True))
    es = [jnp.exp(s - m) for s in ss]
    l = es[0].sum(axis=-1, keepdims=True)
    for e in es[1:]:
        l = l + e.sum(axis=-1, keepdims=True)
    return es, l


def _softmax_attend(q, ks, vs, biases):
    es, l = _softmax_parts(_scores(q, ks, biases))
    o = _dot(es[0].astype(BF16), vs[0])
    for e, v in zip(es[1:], vs[1:]):
        o = o + _dot(e.astype(BF16), v)
    return o * (1.0 / l)


def _diff_attend(q0, q1, k0s, k1s, vs, lam, subln_g, out_scale):
    e0, l0 = _softmax_parts(_scores(q0, k0s, [None] * len(k0s)))
    e1, l1 = _softmax_parts(_scores(q1, k1s, [None] * len(k1s)))
    r0 = 1.0 / l0
    r1 = lam / l1
    y = None
    for a0, a1, v in zip(e0, e1, vs):
        t = _dot((a0 * r0 - a1 * r1).astype(BF16), v)
        y = t if y is None else y + t
    y = y * lax.rsqrt(jnp.mean(y * y, axis=-1, keepdims=True) + RMS_EPS) * subln_g
    return y * out_scale


def _na_kernel(q_ref, k0_ref, k1_ref, k2_ref, v0_ref, v1_ref, v2_ref, kc_ref, vc_ref, bias_ref,
               o_ref):
    for p in range(NA_HEADS // 2):
        ks = [k0_ref[0, p], k1_ref[0, p], k2_ref[0, p], kc_ref[0, p]]
        vs = [v0_ref[0, p], v1_ref[0, p], v2_ref[0, p], vc_ref[0, p]]
        outs = []
        for half, q in enumerate(_head_halves(q_ref[0, p])):
            h = 2 * p + half
            biases = [bias_ref[0, h, :, j * NA_KEY_CHUNK:(j + 1) * NA_KEY_CHUNK] for j in range(NA_N_KEY_CHUNKS)]
            outs.append(_softmax_attend(q, ks, vs, biases + [None]))
        o_ref[0, :, p * LANES:(p + 1) * LANES] = _merge_halves(*outs).astype(BF16)


def _na_bias_table(rpb):
    rows_total = 32
    n_h = rpb.shape[0]
    pad = GRID_W - NA_WIN_W
    rpb_pad = jnp.pad(rpb.astype(F32), ((0, 0), (0, 0), (pad, pad)))
    col_blocks = jnp.stack([rpb_pad[:, :, pad + NA_WIN_W - 1 - qc:pad + NA_WIN_W - 1 - qc + GRID_W]
                            for qc in range(GRID_W)], axis=2)
    q_c = np.arange(GRID_W)
    col_start = np.clip(q_c - NA_WIN_W // 2, 0, GRID_W - NA_WIN_W)
    col_ok = (q_c[None, :] >= col_start[:, None]) & (q_c[None, :] < col_start[:, None] + NA_WIN_W)
    col_blocks = jnp.where(jnp.asarray(col_ok)[None, None], col_blocks, NEG_INF)
    masked = jnp.full((n_h, GRID_W, GRID_W), NEG_INF, F32)
    kinds = []
    for r0 in (0, 4, 24, 28):
        ws = int(np.clip(r0 - NA_WIN_H // 2, 0, rows_total - NA_KEY_ROWS))
        q_rows = []
        for qr in range(r0, r0 + NA_TILE_ROWS):
            row_start = int(np.clip(qr - NA_WIN_H // 2, 0, rows_total - NA_WIN_H))
            blocks = []
            for kr in range(ws, ws + NA_KEY_ROWS):
                inside = row_start <= kr < row_start + NA_WIN_H
                blocks.append(col_blocks[:, kr - qr + NA_WIN_H - 1] if inside else masked)
            q_rows.append(jnp.concatenate(blocks, axis=-1))
        kinds.append(jnp.concatenate(q_rows, axis=1))
    return jnp.stack(kinds, axis=0)


def _na_attention(qa, ka, va, ka_c, va_c, bias_tab):
    bsz, n_p, n_tok, _ = qa.shape
    n_ctx = ka_c.shape[2]
    n_tiles = n_tok // NA_TQ
    last_start = (n_tok - NA_KEY_ROWS * GRID_W) // NA_KEY_CHUNK

    def win(j):
        return lambda t, b: (b, 0, jnp.clip(t - 1, 0, last_start) + j, 0)

    kv_spec = lambda j: pl.BlockSpec((1, n_p, NA_KEY_CHUNK, LANES), win(j))
    ctx_spec = pl.BlockSpec((1, n_p, n_ctx, LANES), lambda t, b: (b, 0, 0, 0))
    bias_map = lambda t, b: (jnp.minimum(t, 1) + jnp.maximum(t - (n_tiles - 3), 0), 0, 0, 0)
    return pl.pallas_call(
        _na_kernel,
        grid=(n_tiles, bsz),
        in_specs=[pl.BlockSpec((1, n_p, NA_TQ, LANES), lambda t, b: (b, 0, t, 0)),
                  kv_spec(0), kv_spec(1), kv_spec(2), kv_spec(0), kv_spec(1), kv_spec(2),
                  ctx_spec, ctx_spec,
                  pl.BlockSpec((1, NA_HEADS, NA_TQ, NA_KEY_ROWS * GRID_W), bias_map)],
        out_specs=pl.BlockSpec((1, NA_TQ, n_p * LANES), lambda t, b: (b, t, 0)),
        out_shape=jax.ShapeDtypeStruct((bsz, n_tok, n_p * LANES), BF16),
        compiler_params=_cparams(("arbitrary", "arbitrary")),
        name="na_attention",
    )(qa, ka, ka, ka, va, va, va, ka_c, va_c, bias_tab)


def _diff_kernel(lam_ref, q_ref, k_ref, kc_ref, v_ref, vc_ref, g_ref, o_ref, *, out_scale, with_latent):
    lam = lam_ref[0, 0]
    for h in range(DIFF_HEADS):
        if with_latent:
            ks, vs = [k_ref[0, h], kc_ref[0, h]], [v_ref[0, h], vc_ref[0, h]]
        else:
            ks, vs = [kc_ref[0, h]], [vc_ref[0, h]]
        q0, q1 = _head_halves(q_ref[0, h])
        y = _diff_attend(q0, q1, ks, ks, vs, lam, g_ref[...], out_scale)
        o_ref[0, :, h * DIFF_VDIM:(h + 1) * DIFF_VDIM] = y.astype(BF16)


def _diff_attention(lam, qb, kb, vb, kb_c, vb_c, subln_g, *, tq, out_scale):
    bsz, _, n_tok, _ = qb.shape
    n_ctx = kb_c.shape[2]
    full = lambda a: pl.BlockSpec((1,) + a.shape[1:], lambda b, i: (b, 0, 0, 0))
    return pl.pallas_call(
        functools.partial(_diff_kernel, out_scale=out_scale, with_latent=True),
        grid=(bsz, n_tok // tq),
        in_specs=[pl.BlockSpec(memory_space=pltpu.SMEM),
                  pl.BlockSpec((1, DIFF_HEADS, tq, LANES), lambda b, i: (b, 0, i, 0)),
                  full(kb), full(kb_c), full(vb), full(vb_c),
                  pl.BlockSpec((1, DIFF_VDIM), lambda b, i: (0, 0))],
        out_specs=pl.BlockSpec((1, tq, DIFF_HEADS * DIFF_VDIM), lambda b, i: (b, i, 0)),
        out_shape=jax.ShapeDtypeStruct((bsz, n_tok, DIFF_HEADS * DIFF_VDIM), BF16),
        compiler_params=_cparams(("arbitrary", "arbitrary")),
        name="diff_attention",
    )(lam, qb, kb, kb_c, vb, vb_c, subln_g.reshape(1, DIFF_VDIM))


def _ctx_attn0_kernel(lam_ref, qa_ref, ka_ref, va_ref, qb_ref, kb_ref, vb_ref, g_ref, ya_ref, yb_ref,
                      *, out_scale):
    for p in range(NA_HEADS // 2):
        outs = [_softmax_attend(q, [ka_ref[0, p]], [va_ref[0, p]], [None]) for q in _head_halves(qa_ref[0, p])]
        ya_ref[0, :, p * LANES:(p + 1) * LANES] = _merge_halves(*outs).astype(BF16)
    _diff_kernel(lam_ref, qb_ref, None, kb_ref, None, vb_ref, g_ref, yb_ref,
                 out_scale=out_scale, with_latent=False)


def _ctx_attention0(lam, qa, ka, va, qb, kb, vb, subln_g, *, out_scale):
    bsz, _, n_ctx, _ = qa.shape
    full = lambda a: pl.BlockSpec((1,) + a.shape[1:], lambda b: (b, 0, 0, 0))
    width = NA_HEADS * HEAD_DIM
    out = jax.ShapeDtypeStruct((bsz, n_ctx, width), BF16)
    return pl.pallas_call(
        functools.partial(_ctx_attn0_kernel, out_scale=out_scale),
        grid=(bsz,),
        in_specs=[pl.BlockSpec(memory_space=pltpu.SMEM),
                  full(qa), full(ka), full(va), full(qb), full(kb), full(vb),
                  pl.BlockSpec((1, DIFF_VDIM), lambda b: (0, 0))],
        out_specs=[pl.BlockSpec((1, n_ctx, width), lambda b: (b, 0, 0))] * 2,
        out_shape=[out, out],
        compiler_params=_cparams(("arbitrary",)),
        name="ctx_attention0",
    )(lam, qa, ka, va, qb, kb, vb, subln_g.reshape(1, DIFF_VDIM))


FFN_CHUNK = FFN_DIM // 2


def _post0_kernel(x_ref, ya_ref, yb_ref, g1_ref, sh2_ref, sc2_ref, g2_ref, wo_ref, w13_ref, w2_ref,
                  ln1g_ref, ln1b_ref, ln2g_ref, ln2b_ref, o_ref):
    half = wo_ref.shape[0] // 2
    y = _dot(ya_ref[0], wo_ref[:half, :]) + _dot(yb_ref[0], wo_ref[half:, :])
    x1 = _layer_norm(DEEPNORM_ALPHA * x_ref[0] + g1_ref[0] * y, ln1g_ref[...], ln1b_ref[...])
    h = (x1 * (1.0 + sc2_ref[0]) + sh2_ref[0]).astype(BF16)
    f = None
    for c in range(FFN_DIM // FFN_CHUNK):
        a = _dot(h, w13_ref[:, c * FFN_CHUNK:(c + 1) * FFN_CHUNK])
        g = _dot(h, w13_ref[:, FFN_DIM + c * FFN_CHUNK:FFN_DIM + (c + 1) * FFN_CHUNK])
        t = _dot((_silu(a) * g).astype(BF16), w2_ref[c * FFN_CHUNK:(c + 1) * FFN_CHUNK, :])
        f = t if f is None else f + t
    o_ref[0] = _layer_norm(DEEPNORM_ALPHA * x1 + g2_ref[0] * f, ln2g_ref[...], ln2b_ref[...])


def _post0(x, ya, yb, g1, sh2, sc2, g2, wo, w13, w2, ln1g, ln1b, ln2g, ln2b, *, tm):
    bsz, n_tok, _ = x.shape
    per_batch = g1.shape[0] > 1
    mod_map = (lambda b, i: (b, 0, 0)) if per_batch else (lambda b, i: (0, 0, 0))
    mod = pl.BlockSpec((1, 1, D_MODEL), mod_map)
    const = lambda a: pl.BlockSpec(a.shape, lambda b, i: (0,) * a.ndim, pipeline_mode=pl.Buffered(1))
    vec = lambda a: a.reshape(1, D_MODEL)
    half = ya.shape[2]
    return pl.pallas_call(
        _post0_kernel,
        grid=(bsz, n_tok // tm),
        in_specs=[pl.BlockSpec((1, tm, D_MODEL), lambda b, i: (b, i, 0)),
                  pl.BlockSpec((1, tm, half), lambda b, i: (b, i, 0)),
                  pl.BlockSpec((1, tm, half), lambda b, i: (b, i, 0)),
                  mod, mod, mod, mod, const(wo), const(w13), const(w2),
                  const(vec(ln1g)), const(vec(ln1b)), const(vec(ln2g)), const(vec(ln2b))],
        out_specs=pl.BlockSpec((1, tm, D_MODEL), lambda b, i: (b, i, 0)),
        out_shape=jax.ShapeDtypeStruct(x.shape, F32),
        compiler_params=_cparams(("arbitrary", "arbitrary")),
        name="post0",
    )(x, ya, yb, g1, sh2, sc2, g2, wo, w13, w2, vec(ln1g), vec(ln1b), vec(ln2g), vec(ln2b))


NORM_CHUNK = 4 * HEAD_DIM


def _inproj1_kernel(x_ref, sh_ref, sc_ref, w_ref, gain_ref, ind_ref, indt_ref, cos_ref, sin_ref, *out_refs,
                    n_norm_chunks, out_layout, rope):
    h = (x_ref[0] * (1.0 + sc_ref[0]) + sh_ref[0]).astype(BF16)
    lane_hi = (lax.broadcasted_iota(jnp.int32, (1, LANES), 1) % HEAD_DIM) >= HEAD_DIM // 2
    targets = iter([(ref, i, dup) for ref, (n, dup) in zip(out_refs, out_layout)
                    for i in range(0, n, 2 if dup else 1)])
    n_chunks = w_ref.shape[1] // NORM_CHUNK
    for c in range(n_chunks):
        cols = slice(c * NORM_CHUNK, (c + 1) * NORM_CHUNK)
        p = _dot(h, w_ref[:, cols])
        if c < n_norm_chunks:
            ms = _dot2(p * p, ind_ref[...]) * (1.0 / HEAD_DIM)
            r = _dot2(lax.rsqrt(ms + RMS_EPS), indt_ref[...])
            p = p * r * gain_ref[:, cols]
            if rope:
                p = jnp.concatenate(
                    [_rope128(p[:, j * LANES:(j + 1) * LANES], cos_ref[...], sin_ref[...], lane_hi)
                     for j in range(NORM_CHUNK // LANES)], axis=1)
        _store_pairs(p, targets)


def _inproj1(x, shift, scale, w, gain, cos, sin, *, tm, out_layout, n_norm_chunks, rope):
    bsz, n_tok, _ = x.shape
    per_batch = shift.shape[0] > 1
    mod_map = (lambda b, i: (b, 0, 0)) if per_batch else (lambda b, i: (0, 0, 0))
    ind_np = np.zeros((NORM_CHUNK, LANES), np.float32)
    ind_np[np.arange(NORM_CHUNK), np.arange(NORM_CHUNK) // HEAD_DIM] = 1.0
    ind = jnp.asarray(ind_np, BF16)
    indt = jnp.asarray(ind_np.T, BF16)
    const = lambda a: pl.BlockSpec(a.shape, lambda b, i: (0,) * a.ndim)
    return pl.pallas_call(
        functools.partial(_inproj1_kernel, n_norm_chunks=n_norm_chunks, out_layout=out_layout, rope=rope),
        grid=(bsz, n_tok // tm),
        in_specs=[pl.BlockSpec((1, tm, D_MODEL), lambda b, i: (b, i, 0)),
                  pl.BlockSpec((1, 1, D_MODEL), mod_map),
                  pl.BlockSpec((1, 1, D_MODEL), mod_map),
                  const(w), const(gain), const(ind), const(indt),
                  pl.BlockSpec((tm, LANES), lambda b, i: (i, 0)),
                  pl.BlockSpec((tm, LANES), lambda b, i: (i, 0))],
        out_specs=[pl.BlockSpec((1, n, tm, LANES), lambda b, i: (b, 0, i, 0)) for n, _ in out_layout],
        out_shape=[jax.ShapeDtypeStruct((bsz, n, n_tok, LANES), BF16) for n, _ in out_layout],
        compiler_params=_cparams(("arbitrary", "arbitrary")),
        name="inproj1",
    )(x, shift, scale, w, gain, ind, indt, cos, sin)


def _gqa_kernel(q_ref, k_ref, kc_ref, v_ref, vc_ref, o_ref):
    for p in range(GQA_Q_HEADS // 2):
        g = 2 * p // GQA_GROUP
        ks, vs = [k_ref[0, g], kc_ref[0, g]], [v_ref[0, g], vc_ref[0, g]]
        outs = [_softmax_attend(q, ks, vs, [None, None]) for q in _head_halves(q_ref[0, p])]
        o_ref[0, :, p * LANES:(p + 1) * LANES] = _merge_halves(*outs).astype(BF16)


def _gqa_attention(q, k, v, k_c, v_c, *, tq):
    bsz, n_p, n_tok, _ = q.shape
    full = lambda a: pl.BlockSpec((1,) + a.shape[1:], lambda b, i: (b, 0, 0, 0))
    return pl.pallas_call(
        _gqa_kernel,
        grid=(bsz, n_tok // tq),
        in_specs=[pl.BlockSpec((1, n_p, tq, LANES), lambda b, i: (b, 0, i, 0)),
                  full(k), full(k_c), full(v), full(v_c)],
        out_specs=pl.BlockSpec((1, tq, n_p * LANES), lambda b, i: (b, i, 0)),
        out_shape=jax.ShapeDtypeStruct((bsz, n_tok, n_p * LANES), BF16),
        compiler_params=_cparams(("arbitrary", "arbitrary")),
        name="gqa_attention",
    )(q, k, k_c, v, v_c)


def _out1_kernel(x_ref, y_ref, g1_ref, sh2_ref, sc2_ref, wo_ref, lng_ref, lnb_ref, rw_ref,
                 x1_ref, hm_ref, route_ref):
    rw_hi, rw_lo = _split(rw_ref[...])
    n_sub = 2
    rows_sub = x_ref.shape[1] // n_sub
    for sub in range(n_sub):
        rows = slice(sub * rows_sub, (sub + 1) * rows_sub)
        y = _dot(y_ref[0, rows, :], wo_ref[...])
        x1 = _layer_norm(DEEPNORM_ALPHA * x_ref[0, rows, :] + g1_ref[0] * y, lng_ref[...], lnb_ref[...])
        x1_ref[0, rows, :] = x1
        hm = x1 * (1.0 + sc2_ref[0]) + sh2_ref[0]
        hm_ref[0, rows, :] = hm
        lane = lax.broadcasted_iota(jnp.int32, (rows_sub, LANES), 1)
        hm_hi, hm_lo = _split(hm)
        logits = _dot(hm_hi, rw_hi) + (_dot(hm_hi, rw_lo) + _dot(hm_lo, rw_hi))
        logits = jnp.where(lane < N_EXPERTS, logits, NEG_INF)
        v1 = logits.max(axis=-1, keepdims=True)
        i1 = jnp.where(logits == v1, lane, LANES).min(axis=-1, keepdims=True)
        rest = jnp.where(lane == i1, NEG_INF, logits)
        v2 = rest.max(axis=-1, keepdims=True)
        i2 = jnp.where(rest == v2, lane, LANES).min(axis=-1, keepdims=True)
        e = jnp.exp(v2 - v1)
        inv = 1.0 / (1.0 + e)
        route_ref[0, rows, :] = jnp.where(
            lane == 0, i1.astype(F32),
            jnp.where(lane == 1, i2.astype(F32), jnp.where(lane == 2, inv, jnp.where(lane == 3, e * inv, 0.0))))


def _out1(x, y, g1, sh2, sc2, wo, lng, lnb, rw, *, tm):
    bsz, n_tok, _ = x.shape
    mod = pl.BlockSpec((1, 1, D_MODEL), lambda b, i: (b, 0, 0))
    tile = pl.BlockSpec((1, tm, D_MODEL), lambda b, i: (b, i, 0))
    const = lambda a: pl.BlockSpec(a.shape, lambda b, i: (0,) * a.ndim)
    vec = lambda a: a.reshape(1, D_MODEL)
    return pl.pallas_call(
        _out1_kernel,
        grid=(bsz, n_tok // tm),
        in_specs=[tile, tile, mod, mod, mod, const(wo), const(vec(lng)), const(vec(lnb)), const(rw)],
        out_specs=[tile, tile, pl.BlockSpec((1, tm, LANES), lambda b, i: (b, i, 0))],
        out_shape=[jax.ShapeDtypeStruct(x.shape, F32), jax.ShapeDtypeStruct(x.shape, F32),
                   jax.ShapeDtypeStruct((bsz, n_tok, LANES), F32)],
        compiler_params=_cparams(("arbitrary", "arbitrary")),
        name="out1",
    )(x, y, g1, sh2, sc2, wo, vec(lng), vec(lnb), rw)


def _moe_kernel(be_ref, nv_ref, gfirst_ref, gnext_ref, sprev_ref, slast_ref, hm_ref, w1_ref, w3_ref, w2_ref,
                y_ref, xg_ref, xb_ref, hid_ref, acc_ref, gsem, ssem):
    del be_ref
    i, j = pl.program_id(0), pl.program_id(1)
    n_valid = nv_ref[0]
    slot = i % 2
    other = 1 - slot

    def gather_row(idx_ref, row, dst_slot):
        return pltpu.make_async_copy(hm_ref.at[pl.ds(idx_ref[0, 0, row], 1)],
                                     xg_ref.at[dst_slot, pl.ds(row, 1)], gsem.at[dst_slot])

    def scatter_row(idx_ref, row, src_slot):
        return pltpu.make_async_copy(acc_ref.at[src_slot, pl.ds(row, 1)],
                                     y_ref.at[pl.ds(idx_ref[0, 0, row], 1)], ssem.at[src_slot])

    def wait_gather(s):
        pltpu.make_async_copy(hm_ref.at[pl.ds(0, MOE_TM)], xg_ref.at[s], gsem.at[s]).wait()

    def wait_scatter(s):
        pltpu.make_async_copy(acc_ref.at[s], y_ref.at[pl.ds(0, MOE_TM)], ssem.at[s]).wait()

    @pl.when(jnp.logical_and(i == 0, j == 0))
    def _():
        acc_ref[1] = jnp.zeros((MOE_TM, D_MODEL), F32)

        def start(r, carry):
            gather_row(gfirst_ref, r, 0).start()
            return carry

        lax.fori_loop(0, MOE_TM, start, 0)
        n_real = y_ref.shape[0] - 2 * MOE_TM
        fill = pltpu.make_async_copy(acc_ref.at[1], y_ref.at[pl.ds(n_real, MOE_TM)], ssem.at[1])
        fill.start()
        fill.wait()

    @pl.when(jnp.logical_and(i < n_valid, j == 0))
    def _():
        wait_gather(slot)
        xb_ref[...] = xg_ref[slot].astype(BF16)

        @pl.when(i >= 1)
        def _():
            wait_scatter(slot)

        acc_ref[slot] = jnp.zeros((MOE_TM, D_MODEL), F32)

    @pl.when(i < n_valid)
    def _():
        base = j * MOE_ROWS_PER_STEP
        n_up = MOE_TF // MOE_CHUNK
        n_down = D_MODEL // MOE_CHUNK
        per_group = -(-MOE_ROWS_PER_STEP // (n_up + n_down))
        rows = iter(range(MOE_ROWS_PER_STEP))

        def copy_group():
            for _ in range(per_group):
                r = next(rows, None)
                if r is not None:
                    gather_row(gnext_ref, base + r, other).start(priority=r % 2)
                    scatter_row(sprev_ref, base + r, other).start(priority=r % 2)

        cols = lambda n: slice(n * MOE_CHUNK, (n + 1) * MOE_CHUNK)
        up = lambda n: (_dot(xb_ref[...], w1_ref[0, :, cols(n)]), _dot(xb_ref[...], w3_ref[0, :, cols(n)]))
        pending = up(0)
        for n in range(n_up):
            following = up(n + 1) if n + 1 < n_up else None
            a, g = pending
            hid_ref[:, cols(n)] = (_silu(a) * g).astype(BF16)
            copy_group()
            pending = following
        down = lambda n: _dot(hid_ref[...], w2_ref[0, :, cols(n)])
        pending = down(0)
        for n in range(n_down):
            following = down(n + 1) if n + 1 < n_down else None
            acc_ref[slot, :, cols(n)] += pending
            copy_group()
            pending = following

    @pl.when(jnp.logical_and(i == n_valid - 1, j == MOE_NF - 1))
    def _():
        def start(r, carry):
            scatter_row(slast_ref, r, slot).start()
            return carry

        lax.fori_loop(0, MOE_TM, start, 0)
        wait_gather(other)
        wait_scatter(other)
        wait_scatter(slot)


def _moe_experts(hm_packed, plan, w13, w2):
    block_e, n_valid, gidx, sidx, n_out_rows = plan
    n_blocks = block_e.shape[0]
    smem_rows = lambda index_map: pl.BlockSpec((1, 1, MOE_TM), index_map, memory_space=pltpu.SMEM)
    return pl.pallas_call(
        _moe_kernel,
        grid_spec=pltpu.PrefetchScalarGridSpec(
            num_scalar_prefetch=2,
            grid=(n_blocks, MOE_NF),
            in_specs=[smem_rows(lambda i, j, be, nv: (0, 0, 0)),
                      smem_rows(lambda i, j, be, nv: (i + 1, 0, 0)),
                      smem_rows(lambda i, j, be, nv: (i, 0, 0)),
                      smem_rows(lambda i, j, be, nv: (nv[0], 0, 0)),
                      pl.BlockSpec(memory_space=pl.ANY),
                      pl.BlockSpec((1, D_MODEL, MOE_TF), lambda i, j, be, nv: (be[i], 0, j)),
                      pl.BlockSpec((1, D_MODEL, MOE_TF), lambda i, j, be, nv: (be[i], 0, MOE_NF + j)),
                      pl.BlockSpec((1, MOE_TF, D_MODEL), lambda i, j, be, nv: (be[i], j, 0))],
            out_specs=pl.BlockSpec(memory_space=pl.ANY),
            scratch_shapes=[pltpu.VMEM((2, MOE_TM, D_MODEL), F32),
                            pltpu.VMEM((MOE_TM, D_MODEL), BF16),
                            pltpu.VMEM((MOE_TM, MOE_TF), BF16),
                            pltpu.VMEM((2, MOE_TM, D_MODEL), F32),
                            pltpu.SemaphoreType.DMA((2,)),
                            pltpu.SemaphoreType.DMA((2,))]),
        out_shape=jax.ShapeDtypeStruct((n_out_rows, D_MODEL), F32),
        compiler_params=_cparams(("arbitrary", "arbitrary")),
        name="moe_experts",
    )(block_e, n_valid, gidx, gidx, sidx, sidx, hm_packed, w13, w13, w2)


def _combine_kernel(x1_ref, y0_ref, y1_ref, route_ref, g2_ref, lng_ref, lnb_ref, o_ref):
    route = route_ref[...]
    m = y0_ref[...] * route[:, 2:3] + y1_ref[...] * route[:, 3:4]
    o_ref[...] = _layer_norm(DEEPNORM_ALPHA * x1_ref[...] + g2_ref[0] * m, lng_ref[...], lnb_ref[...])


def _combine(x1, y_rows, route, g2, lng, lnb, *, tm, tiles_per_batch):
    n_tok = x1.shape[0]
    vec = lambda a: a.reshape(1, D_MODEL)
    const = lambda a: pl.BlockSpec(a.shape, lambda i: (0,) * a.ndim)
    return pl.pallas_call(
        _combine_kernel,
        grid=(n_tok // tm,),
        in_specs=[pl.BlockSpec((tm, D_MODEL), lambda i: (i, 0)),
                  pl.BlockSpec((tm, D_MODEL), lambda i: (i, 0)),
                  pl.BlockSpec((tm, D_MODEL), lambda i: (i + n_tok // tm, 0)),
                  pl.BlockSpec((tm, LANES), lambda i: (i, 0)),
                  pl.BlockSpec((1, 1, D_MODEL), lambda i: (i // tiles_per_batch, 0, 0)),
                  const(vec(lng)), const(vec(lnb))],
        out_specs=pl.BlockSpec((tm, D_MODEL), lambda i: (i, 0)),
        out_shape=jax.ShapeDtypeStruct(x1.shape, F32),
        compiler_params=_cparams(("arbitrary",)),
        name="moe_combine",
    )(x1, y_rows, y_rows, route, g2, vec(lng), vec(lnb))


def _routing_plan(route, n_tok):
    n_pairs = n_tok * TOP_K
    flat_e = route[:, :TOP_K].astype(jnp.int32).reshape(-1)
    pair = jnp.arange(n_pairs, dtype=jnp.int32)
    sorted_key = jnp.sort(flat_e * n_pairs + pair)
    order = sorted_key - (sorted_key // n_pairs) * n_pairs
    experts = jnp.arange(N_EXPERTS, dtype=jnp.int32)
    counts = jnp.sum((flat_e[:, None] == experts[None, :]).astype(jnp.int32), axis=0)
    first_pair = jnp.cumsum(counts) - counts
    blocks_per = (counts + MOE_TM - 1) // MOE_TM
    block_end = jnp.cumsum(blocks_per)
    block_start = block_end - blocks_per
    n_blocks = -(-(n_pairs + N_EXPERTS * (MOE_TM - 1)) // MOE_TM)
    blk = jnp.arange(n_blocks, dtype=jnp.int32)
    block_e = jnp.minimum(jnp.sum((block_end[None, :] <= blk[:, None]).astype(jnp.int32), axis=1), N_EXPERTS - 1)
    n_valid = block_end[-1]
    sel = (block_e[:, None] == experts[None, :]).astype(jnp.int32)
    pick = lambda v: jnp.sum(sel * v[None, :], axis=1)
    offset = (blk - pick(block_start)) * MOE_TM
    src = jnp.clip(pick(first_pair) + offset, 0, n_pairs)
    r = jnp.arange(MOE_TM, dtype=jnp.int32)[None, :]
    pairs = jnp.take(order, jnp.minimum(src[:, None] + r, n_pairs - 1), axis=0)
    used = jnp.logical_and((blk < n_valid)[:, None], offset[:, None] + r < pick(counts)[:, None])
    spare = n_pairs + (blk % 2)[:, None] * MOE_TM + r
    token, choice = pairs // TOP_K, pairs % TOP_K
    gidx = jnp.where(used, token, 0)
    sidx = jnp.where(used, choice * n_tok + token, spare)
    gidx = jnp.concatenate([gidx, jnp.zeros((1, MOE_TM), jnp.int32)], axis=0)
    sidx = jnp.concatenate([n_pairs + MOE_TM + r, sidx], axis=0)
    shape3 = (n_blocks + 1, 1, MOE_TM)
    return (block_e.astype(jnp.int32), n_valid.astype(jnp.int32).reshape(1), gidx.reshape(shape3),
            sidx.reshape(shape3), n_pairs + 2 * MOE_TM)


def _rope_tables(n_tok, n_identity=0):
    t = jnp.arange(n_tok, dtype=jnp.int32)
    n_freq = HEAD_DIM // 4
    inv_freq = ROPE_THETA ** (-jnp.arange(n_freq, dtype=F32) / n_freq)
    ang = jnp.concatenate([(t // GRID_W).astype(F32)[:, None] * inv_freq,
                           (t % GRID_W).astype(F32)[:, None] * inv_freq], axis=-1)
    cos, sin = jnp.cos(ang), jnp.sin(ang)
    cos128 = jnp.tile(cos, (1, 4))
    sin128 = jnp.tile(jnp.concatenate([-sin, sin], axis=-1), (1, 2))
    return cos128, sin128


def _mods(cond_rows, ada_w, ada_b, bsz):
    out = _ada_mods(cond_rows, ada_w, ada_b)
    lat = [m.reshape(bsz, 1, D_MODEL) for m in jnp.split(out[:bsz], 6, axis=-1)]
    ctx = [m.reshape(1, 1, D_MODEL) for m in jnp.split(out[bsz:bsz + 1], 6, axis=-1)]
    return lat, ctx


def kernel(x, c, ctx, c_ctx, l0_ada_w, l0_ada_b, l0_w_in, l0_rpb, l0_lambda_qk, l0_subln_g, l0_w_out, l0_ln1_g, l0_ln1_b, l0_ffn_w13, l0_ffn_w2, l0_ln2_g, l0_ln2_b, l1_ada_w, l1_ada_b, l1_w_in, l1_q_norm_g, l1_k_norm_g, l1_w_out, l1_ln1_g, l1_ln1_b, l1_router_w, l1_moe_w13, l1_moe_w2, l1_ln2_g, l1_ln2_b):
    bsz, n_tok, _ = x.shape
    n_ctx = ctx.shape[1]
    cond_rows = jnp.concatenate([c, c_ctx[None, :], jnp.zeros((32 - bsz - 1, D_MODEL), F32)], axis=0)
    cos, sin = _rope_tables(n_tok)
    ones = jnp.ones((n_ctx, LANES), F32)
    zeros = jnp.zeros((n_ctx, LANES), F32)

    (sh1, sc1, g1, sh2, sc2, g2), (csh1, csc1, cg1, csh2, csc2, cg2) = _mods(cond_rows, l0_ada_w, l0_ada_b, bsz)
    w_in0 = l0_w_in.astype(BF16)
    qa, ka, va, qb, kb, vb = _inproj0(x, sh1, sc1, w_in0, cos, sin, tm=512, rope=True)
    qa_c, ka_c, va_c, qb_c, kb_c, vb_c = _inproj0(ctx, csh1, csc1, w_in0, ones, zeros, tm=n_ctx, rope=False)
    lam_init = 0.8 - 0.6 * math.exp(-0.3 * 0)
    lq = l0_lambda_qk.astype(F32)
    lam = (jnp.exp(jnp.sum(lq[0] * lq[1])) - jnp.exp(jnp.sum(lq[2] * lq[3])) + lam_init).reshape(1, 1)
    y_a = _na_attention(qa, ka, va, ka_c, va_c, _na_bias_table(l0_rpb))
    y_b = _diff_attention(lam, qb, kb, vb, kb_c, vb_c, l0_subln_g, tq=256, out_scale=1.0 - lam_init)
    ya_c, yb_c = _ctx_attention0(lam, qa_c, ka_c, va_c, qb_c, kb_c, vb_c, l0_subln_g, out_scale=1.0 - lam_init)
    wo0, w13_0, w2_0 = l0_w_out.astype(BF16), l0_ffn_w13.astype(BF16), l0_ffn_w2.astype(BF16)
    ln0 = (l0_ln1_g, l0_ln1_b, l0_ln2_g, l0_ln2_b)
    x = _post0(x, y_a, y_b, g1, sh2, sc2, g2, wo0, w13_0, w2_0, *ln0, tm=512)
    ctx = _post0(ctx, ya_c, yb_c, cg1, csh2, csc2, cg2, wo0, w13_0, w2_0, *ln0, tm=n_ctx)

    (sh1, sc1, g1, sh2, sc2, g2), (csh1, csc1, _, _, _, _) = _mods(cond_rows, l1_ada_w, l1_ada_b, bsz)
    w_in1 = l1_w_in.astype(BF16)
    n_q, n_kv = GQA_Q_HEADS * HEAD_DIM, GQA_KV_HEADS * HEAD_DIM
    gain = jnp.concatenate([jnp.tile(l1_q_norm_g, GQA_Q_HEADS) * QK_SCALE,
                            jnp.tile(l1_k_norm_g, GQA_KV_HEADS)]).reshape(1, n_q + n_kv)
    q, k, v = _inproj1(x, sh1, sc1, w_in1, gain, cos, sin, tm=512,
                       out_layout=((GQA_Q_HEADS // 2, False), (GQA_KV_HEADS, True), (GQA_KV_HEADS, True)),
                       n_norm_chunks=(n_q + n_kv) // NORM_CHUNK, rope=True)
    k_c, v_c = _inproj1(ctx, csh1, csc1, w_in1[:, n_q:], gain[:, n_q:], ones, zeros, tm=n_ctx,
                        out_layout=((GQA_KV_HEADS, True), (GQA_KV_HEADS, True)),
                        n_norm_chunks=n_kv // NORM_CHUNK, rope=False)
    y = _gqa_attention(q, k, v, k_c, v_c, tq=256)
    rw = jnp.pad(l1_router_w.astype(F32), ((0, 0), (0, LANES - N_EXPERTS)))
    x1, hm, route = _out1(x, y, g1, sh2, sc2, l1_w_out.astype(BF16), l1_ln1_g, l1_ln1_b, rw, tm=512)
    n_all = bsz * n_tok
    x1, hm, route = x1.reshape(n_all, D_MODEL), hm.reshape(n_all, D_MODEL), route.reshape(n_all, LANES)
    y_rows = _moe_experts(hm, _routing_plan(route, n_all), l1_moe_w13.astype(BF16), l1_moe_w2.astype(BF16))
    out = _combine(x1, y_rows, route, g2, l1_ln2_g, l1_ln2_b, tm=512, tiles_per_batch=n_tok // 512)
    return out.reshape(bsz, n_tok, D_MODEL)
```

```python
import functools
import math

import jax
import jax.numpy as jnp
import numpy as np
from jax import lax
from jax.experimental import pallas as pl
from jax.experimental.pallas import tpu as pltpu

F32 = jnp.float32
BF16 = jnp.bfloat16

D_MODEL = 1024
DEPTH = 2
GRID_W = 64
GRID_ROWS = 32
HEAD_DIM = 64
ROPE_THETA = 10000.0
NA_HEADS = 8
NA_WIN_H = 8
NA_WIN_W = 16
DIFF_HEADS = 4
DIFF_VDIM = 2 * HEAD_DIM
GQA_Q_HEADS = 16
GQA_KV_HEADS = 4
GQA_GROUP = GQA_Q_HEADS // GQA_KV_HEADS
FFN_DIM = 2816
N_EXPERTS = 8
TOP_K = 2
EXPERT_DIM = 3584
DEEPNORM_ALPHA = (2 * DEPTH) ** 0.25
LN_EPS = 1e-5
RMS_EPS = 1e-6
NEG_INF = -1e30
QK_SCALE = HEAD_DIM ** -0.5

LANES = 128
VMEM_LIMIT = 56 * 1024 * 1024

NA_TILE_ROWS = 4
NA_KEY_ROWS = 12
NA_TQ = NA_TILE_ROWS * GRID_W
NA_KEY_CHUNK = NA_TQ
NA_N_KEY_CHUNKS = NA_KEY_ROWS * GRID_W // NA_KEY_CHUNK

TOKEN_TILE = 512
ATTN_TQ = 256
COND_ROWS = 32

MOE_TM = 1024
MOE_TF = 1792
MOE_CHUNK = 256
MOE_NF = EXPERT_DIM // MOE_TF
MOE_ROWS_PER_STEP = MOE_TM // MOE_NF


def _cparams(sem, vmem=VMEM_LIMIT):
    return pltpu.CompilerParams(dimension_semantics=sem, vmem_limit_bytes=vmem)


def _dot(a, b):
    return jnp.dot(a, b, preferred_element_type=F32)


def _dot_t(a, b):
    return lax.dot_general(a, b, (((1,), (1,)), ((), ())), preferred_element_type=F32)


def _split(a):
    hi = a.astype(BF16)
    lo = (a - hi.astype(F32)).astype(BF16)
    return hi, lo


def _dot3(a, b):
    a_hi, a_lo = _split(a)
    b_hi, b_lo = _split(b)
    return _dot(a_hi, b_hi) + (_dot(a_hi, b_lo) + _dot(a_lo, b_hi))


def _dot2(a, b_exact):
    a_hi, a_lo = _split(a)
    return _dot(a_hi, b_exact) + _dot(a_lo, b_exact)


def _layer_norm(z, g, b):
    mu = jnp.mean(z, axis=-1, keepdims=True)
    zc = z - mu
    var = jnp.mean(zc * zc, axis=-1, keepdims=True)
    return zc * lax.rsqrt(var + LN_EPS) * g + b


def _silu(a):
    return a * jax.nn.sigmoid(a)


def _rope128(y, cos, sin_signed, lane_hi):
    swapped = jnp.where(lane_hi, pltpu.roll(y, 32, 1), pltpu.roll(y, 96, 1))
    return y * cos + swapped * sin_signed


ROW_TILE = 8
assert D_MODEL == ROW_TILE * LANES


def _to_row_tiles(x, ref):
    for s in range(ROW_TILE):
        ref[pl.ds(s, x.shape[0], stride=ROW_TILE), :] = x[:, s * LANES:(s + 1) * LANES]


def _from_row_tiles(ref, n):
    return jnp.concatenate([ref[pl.ds(s, n, stride=ROW_TILE), :] for s in range(ROW_TILE)], axis=1)


def _head_halves(pair):
    lane = lax.broadcasted_iota(jnp.int32, pair.shape, 1)
    zero = jnp.zeros_like(pair)
    return jnp.where(lane < HEAD_DIM, pair, zero), jnp.where(lane >= HEAD_DIM, pair, zero)


def _merge_halves(lo, hi):
    lane = lax.broadcasted_iota(jnp.int32, lo.shape, 1)
    return jnp.where(lane < HEAD_DIM, lo, hi)


def _store_pairs(y, refs_iter):
    lane = lax.broadcasted_iota(jnp.int32, (1, LANES), 1)
    for c in range(y.shape[1] // LANES):
        slab = y[:, c * LANES:(c + 1) * LANES]
        ref, idx, dup = next(refs_iter)
        if dup:
            rolled = pltpu.roll(slab, HEAD_DIM, 1)
            ref[0, idx] = jnp.where(lane < HEAD_DIM, slab, rolled).astype(BF16)
            ref[0, idx + 1] = jnp.where(lane < HEAD_DIM, rolled, slab).astype(BF16)
        else:
            ref[0, idx] = slab.astype(BF16)


def _ada_kernel(c_ref, w_ref, b_ref, o_ref):
    o_ref[...] = _dot3(_silu(c_ref[...]), w_ref[...]) + b_ref[...]


def _ada_mods(cond, w, b):
    rows, tn = cond.shape[0], 1536
    n_out = w.shape[1]
    return pl.pallas_call(
        _ada_kernel,
        grid=(n_out // tn,),
        in_specs=[pl.BlockSpec((rows, D_MODEL), lambda j: (0, 0)),
                  pl.BlockSpec((D_MODEL, tn), lambda j: (0, j)),
                  pl.BlockSpec((1, tn), lambda j: (0, j))],
        out_specs=pl.BlockSpec((rows, tn), lambda j: (0, j)),
        out_shape=jax.ShapeDtypeStruct((rows, n_out), F32),
        compiler_params=_cparams(("arbitrary",)),
        name="ada_mods",
    )(cond, w, b.reshape(1, n_out))


def _inproj0_kernel(x_ref, sh_ref, sc_ref, w_ref, cos_ref, sin_ref,
                    qa_ref, ka_ref, va_ref, qb_ref, kb_ref, vb_ref, *, rope):
    h = (x_ref[0] * (1.0 + sc_ref[0]) + sh_ref[0]).astype(BF16)
    lane_hi = (lax.broadcasted_iota(jnp.int32, (1, LANES), 1) % HEAD_DIM) >= HEAD_DIM // 2
    sec = NA_HEADS * HEAD_DIM
    plan = ((qa_ref, QK_SCALE, False), (ka_ref, None, False), (va_ref, None, False),
            (qb_ref, QK_SCALE, True), (kb_ref, None, True))
    for s, (ref, scale, rotary) in enumerate(plan):
        p = _dot(h, w_ref[:, s * sec:(s + 1) * sec])
        if scale is not None:
            p = p * scale
        for c in range(sec // LANES):
            y = p[:, c * LANES:(c + 1) * LANES]
            if rotary and rope:
                y = _rope128(y, cos_ref[...], sin_ref[...], lane_hi)
            ref[0, c] = y.astype(BF16)
    p = _dot(h, w_ref[:, 5 * sec:6 * sec])
    for hh in range(DIFF_HEADS):
        vb_ref[0, hh] = p[:, hh * DIFF_VDIM:(hh + 1) * DIFF_VDIM].astype(BF16)


def _inproj0(x, shift, scale, w, cos, sin, *, tm, rope):
    bsz, n_tok, _ = x.shape
    per_batch = shift.shape[0] > 1
    mod_map = (lambda b, i: (b, 0, 0)) if per_batch else (lambda b, i: (0, 0, 0))
    n_pairs = NA_HEADS // 2
    out_shape = jax.ShapeDtypeStruct((bsz, n_pairs, n_tok, LANES), BF16)
    out_spec = pl.BlockSpec((1, n_pairs, tm, LANES), lambda b, i: (b, 0, i, 0))
    return pl.pallas_call(
        functools.partial(_inproj0_kernel, rope=rope),
        grid=(bsz, n_tok // tm),
        in_specs=[pl.BlockSpec((1, tm, D_MODEL), lambda b, i: (b, i, 0)),
                  pl.BlockSpec((1, 1, D_MODEL), mod_map),
                  pl.BlockSpec((1, 1, D_MODEL), mod_map),
                  pl.BlockSpec(w.shape, lambda b, i: (0, 0)),
                  pl.BlockSpec((tm, LANES), lambda b, i: (i, 0)),
                  pl.BlockSpec((tm, LANES), lambda b, i: (i, 0))],
        out_specs=[out_spec] * 6,
        out_shape=[out_shape] * 6,
        compiler_params=_cparams(("arbitrary", "arbitrary")),
        name="inproj0",
    )(x, shift, scale, w, cos, sin)


def _scores(q, ks, biases):
    out = []
    for k, bias in zip(ks, biases):
        s = _dot_t(q, k)
        out.append(s if bias is None else s + bias)
    return out


def _softmax_parts(ss):
    m = ss[0].max(axis=-1, keepdims=True)
    for s in ss[1:]:
        m = jnp.maximum(m, s.max(axis=-1, keepdims=True))
    es = [jnp.exp(s - m) for s in ss]
    l = es[0].sum(axis=-1, keepdims=True)
    for e in es[1:]:
        l = l + e.sum(axis=-1, keepdims=True)
    return es, l


def _softmax_attend(q, ks, vs, biases):
    es, l = _softmax_parts(_scores(q, ks, biases))
    o = _dot(es[0].astype(BF16), vs[0])
    for e, v in zip(es[1:], vs[1:]):
        o = o + _dot(e.astype(BF16), v)
    return o * (1.0 / l)


def _diff_attend(q0, q1, k0s, k1s, vs, lam, subln_g, out_scale):
    e0, l0 = _softmax_parts(_scores(q0, k0s, [None] * len(k0s)))
    e1, l1 = _softmax_parts(_scores(q1, k1s, [None] * len(k1s)))
    c = lam * l0 / l1
    y = None
    for a0, a1, v in zip(e0, e1, vs):
        t = _dot((a0 - a1 * c).astype(BF16), v)
        y = t if y is None else y + t
    y = y * (1.0 / l0)
    y = y * lax.rsqrt(jnp.mean(y * y, axis=-1, keepdims=True) + RMS_EPS) * subln_g
    return y * out_scale


def _na_kernel(q_ref, k0_ref, k1_ref, k2_ref, v0_ref, v1_ref, v2_ref, kc_ref, vc_ref, bias_ref,
               o_ref):
    for p in range(NA_HEADS // 2):
        ks = [k0_ref[0, p], k1_ref[0, p], k2_ref[0, p], kc_ref[0, p]]
        vs = [v0_ref[0, p], v1_ref[0, p], v2_ref[0, p], vc_ref[0, p]]
        outs = []
        for half, q in enumerate(_head_halves(q_ref[0, p])):
            h = 2 * p + half
            biases = [bias_ref[0, h, :, j * NA_KEY_CHUNK:(j + 1) * NA_KEY_CHUNK] for j in range(NA_N_KEY_CHUNKS)]
            outs.append(_softmax_attend(q, ks, vs, biases + [None]))
        o_ref[0, :, p * LANES:(p + 1) * LANES] = _merge_halves(*outs).astype(BF16)


def _na_bias_table(rpb):
    rows_total = GRID_ROWS
    n_h = rpb.shape[0]
    pad = GRID_W - NA_WIN_W
    rpb_pad = jnp.pad(rpb.astype(F32), ((0, 0), (0, 0), (pad, pad)))
    col_blocks = jnp.stack([rpb_pad[:, :, pad + NA_WIN_W - 1 - qc:pad + NA_WIN_W - 1 - qc + GRID_W]
                            for qc in range(GRID_W)], axis=2)
    q_c = np.arange(GRID_W)
    col_start = np.clip(q_c - NA_WIN_W // 2, 0, GRID_W - NA_WIN_W)
    col_ok = (q_c[None, :] >= col_start[:, None]) & (q_c[None, :] < col_start[:, None] + NA_WIN_W)
    col_blocks = jnp.where(jnp.asarray(col_ok)[None, None], col_blocks, NEG_INF)
    masked = jnp.full((n_h, GRID_W, GRID_W), NEG_INF, F32)
    kinds = []
    for r0 in (0, NA_TILE_ROWS, GRID_ROWS - 2 * NA_TILE_ROWS, GRID_ROWS - NA_TILE_ROWS):
        ws = int(np.clip(r0 - NA_WIN_H // 2, 0, rows_total - NA_KEY_ROWS))
        q_rows = []
        for qr in range(r0, r0 + NA_TILE_ROWS):
            row_start = int(np.clip(qr - NA_WIN_H // 2, 0, rows_total - NA_WIN_H))
            blocks = []
            for kr in range(ws, ws + NA_KEY_ROWS):
                inside = row_start <= kr < row_start + NA_WIN_H
                blocks.append(col_blocks[:, kr - qr + NA_WIN_H - 1] if inside else masked)
            q_rows.append(jnp.concatenate(blocks, axis=-1))
        kinds.append(jnp.concatenate(q_rows, axis=1))
    return jnp.stack(kinds, axis=0)


def _na_attention(qa, ka, va, ka_c, va_c, bias_tab):
    bsz, n_p, n_tok, _ = qa.shape
    n_ctx = ka_c.shape[2]
    n_tiles = n_tok // NA_TQ
    last_start = (n_tok - NA_KEY_ROWS * GRID_W) // NA_KEY_CHUNK

    def win(j):
        return lambda t, b: (b, 0, jnp.clip(t - 1, 0, last_start) + j, 0)

    kv_spec = lambda j: pl.BlockSpec((1, n_p, NA_KEY_CHUNK, LANES), win(j))
    ctx_spec = pl.BlockSpec((1, n_p, n_ctx, LANES), lambda t, b: (b, 0, 0, 0))
    bias_map = lambda t, b: (jnp.minimum(t, 1) + jnp.maximum(t - (n_tiles - 3), 0), 0, 0, 0)
    return pl.pallas_call(
        _na_kernel,
        grid=(n_tiles, bsz),
        in_specs=[pl.BlockSpec((1, n_p, NA_TQ, LANES), lambda t, b: (b, 0, t, 0)),
                  kv_spec(0), kv_spec(1), kv_spec(2), kv_spec(0), kv_spec(1), kv_spec(2),
                  ctx_spec, ctx_spec,
                  pl.BlockSpec((1, NA_HEADS, NA_TQ, NA_KEY_ROWS * GRID_W), bias_map)],
        out_specs=pl.BlockSpec((1, NA_TQ, n_p * LANES), lambda t, b: (b, t, 0)),
        out_shape=jax.ShapeDtypeStruct((bsz, n_tok, n_p * LANES), BF16),
        compiler_params=_cparams(("arbitrary", "arbitrary")),
        name="na_attention",
    )(qa, ka, ka, ka, va, va, va, ka_c, va_c, bias_tab)


def _diff_kernel(lam_ref, q_ref, k_ref, kc_ref, v_ref, vc_ref, g_ref, o_ref, *, out_scale, with_latent):
    lam = lam_ref[0, 0]
    for h in range(DIFF_HEADS):
        if with_latent:
            ks, vs = [k_ref[0, h], kc_ref[0, h]], [v_ref[0, h], vc_ref[0, h]]
        else:
            ks, vs = [kc_ref[0, h]], [vc_ref[0, h]]
        q0, q1 = _head_halves(q_ref[0, h])
        y = _diff_attend(q0, q1, ks, ks, vs, lam, g_ref[...], out_scale)
        o_ref[0, :, h * DIFF_VDIM:(h + 1) * DIFF_VDIM] = y.astype(BF16)


def _diff_attention(lam, qb, kb, vb, kb_c, vb_c, subln_g, *, tq, out_scale):
    bsz, _, n_tok, _ = qb.shape
    n_ctx = kb_c.shape[2]
    full = lambda a: pl.BlockSpec((1,) + a.shape[1:], lambda b, i: (b, 0, 0, 0))
    return pl.pallas_call(
        functools.partial(_diff_kernel, out_scale=out_scale, with_latent=True),
        grid=(bsz, n_tok // tq),
        in_specs=[pl.BlockSpec(memory_space=pltpu.SMEM),
                  pl.BlockSpec((1, DIFF_HEADS, tq, LANES), lambda b, i: (b, 0, i, 0)),
                  full(kb), full(kb_c), full(vb), full(vb_c),
                  pl.BlockSpec((1, DIFF_VDIM), lambda b, i: (0, 0))],
        out_specs=pl.BlockSpec((1, tq, DIFF_HEADS * DIFF_VDIM), lambda b, i: (b, i, 0)),
        out_shape=jax.ShapeDtypeStruct((bsz, n_tok, DIFF_HEADS * DIFF_VDIM), BF16),
        compiler_params=_cparams(("arbitrary", "arbitrary")),
        name="diff_attention",
    )(lam, qb, kb, kb_c, vb, vb_c, subln_g.reshape(1, DIFF_VDIM))


def _ctx_attn0_kernel(lam_ref, qa_ref, ka_ref, va_ref, qb_ref, kb_ref, vb_ref, g_ref, ya_ref, yb_ref,
                      *, out_scale):
    for p in range(NA_HEADS // 2):
        outs = [_softmax_attend(q, [ka_ref[0, p]], [va_ref[0, p]], [None]) for q in _head_halves(qa_ref[0, p])]
        ya_ref[0, :, p * LANES:(p + 1) * LANES] = _merge_halves(*outs).astype(BF16)
    _diff_kernel(lam_ref, qb_ref, None, kb_ref, None, vb_ref, g_ref, yb_ref,
                 out_scale=out_scale, with_latent=False)


def _ctx_attention0(lam, qa, ka, va, qb, kb, vb, subln_g, *, out_scale):
    bsz, _, n_ctx, _ = qa.shape
    full = lambda a: pl.BlockSpec((1,) + a.shape[1:], lambda b: (b, 0, 0, 0))
    width = NA_HEADS * HEAD_DIM
    out = jax.ShapeDtypeStruct((bsz, n_ctx, width), BF16)
    return pl.pallas_call(
        functools.partial(_ctx_attn0_kernel, out_scale=out_scale),
        grid=(bsz,),
        in_specs=[pl.BlockSpec(memory_space=pltpu.SMEM),
                  full(qa), full(ka), full(va), full(qb), full(kb), full(vb),
                  pl.BlockSpec((1, DIFF_VDIM), lambda b: (0, 0))],
        out_specs=[pl.BlockSpec((1, n_ctx, width), lambda b: (b, 0, 0))] * 2,
        out_shape=[out, out],
        compiler_params=_cparams(("arbitrary",)),
        name="ctx_attention0",
    )(lam, qa, ka, va, qb, kb, vb, subln_g.reshape(1, DIFF_VDIM))


FFN_CHUNK = FFN_DIM // 2


def _post0_kernel(x_ref, ya_ref, yb_ref, g1_ref, sh2_ref, sc2_ref, g2_ref, wo_ref, w13_ref, w2_ref,
                  ln1g_ref, ln1b_ref, ln2g_ref, ln2b_ref, o_ref):
    half = wo_ref.shape[0] // 2
    y = _dot(ya_ref[0], wo_ref[:half, :]) + _dot(yb_ref[0], wo_ref[half:, :])
    x1 = _layer_norm(DEEPNORM_ALPHA * x_ref[0] + g1_ref[0] * y, ln1g_ref[...], ln1b_ref[...])
    h = (x1 * (1.0 + sc2_ref[0]) + sh2_ref[0]).astype(BF16)
    f = None
    for c in range(FFN_DIM // FFN_CHUNK):
        a = _dot(h, w13_ref[:, c * FFN_CHUNK:(c + 1) * FFN_CHUNK])
        g = _dot(h, w13_ref[:, FFN_DIM + c * FFN_CHUNK:FFN_DIM + (c + 1) * FFN_CHUNK])
        t = _dot((_silu(a) * g).astype(BF16), w2_ref[c * FFN_CHUNK:(c + 1) * FFN_CHUNK, :])
        f = t if f is None else f + t
    o_ref[0] = _layer_norm(DEEPNORM_ALPHA * x1 + g2_ref[0] * f, ln2g_ref[...], ln2b_ref[...])


def _post0(x, ya, yb, g1, sh2, sc2, g2, wo, w13, w2, ln1g, ln1b, ln2g, ln2b, *, tm):
    bsz, n_tok, _ = x.shape
    per_batch = g1.shape[0] > 1
    mod_map = (lambda b, i: (b, 0, 0)) if per_batch else (lambda b, i: (0, 0, 0))
    mod = pl.BlockSpec((1, 1, D_MODEL), mod_map)
    const = lambda a: pl.BlockSpec(a.shape, lambda b, i: (0,) * a.ndim, pipeline_mode=pl.Buffered(1))
    vec = lambda a: a.reshape(1, D_MODEL)
    half = ya.shape[2]
    return pl.pallas_call(
        _post0_kernel,
        grid=(bsz, n_tok // tm),
        in_specs=[pl.BlockSpec((1, tm, D_MODEL), lambda b, i: (b, i, 0)),
                  pl.BlockSpec((1, tm, half), lambda b, i: (b, i, 0)),
                  pl.BlockSpec((1, tm, half), lambda b, i: (b, i, 0)),
                  mod, mod, mod, mod, const(wo), const(w13), const(w2),
                  const(vec(ln1g)), const(vec(ln1b)), const(vec(ln2g)), const(vec(ln2b))],
        out_specs=pl.BlockSpec((1, tm, D_MODEL), lambda b, i: (b, i, 0)),
        out_shape=jax.ShapeDtypeStruct(x.shape, F32),
        compiler_params=_cparams(("arbitrary", "arbitrary")),
        name="post0",
    )(x, ya, yb, g1, sh2, sc2, g2, wo, w13, w2, vec(ln1g), vec(ln1b), vec(ln2g), vec(ln2b))


NORM_CHUNK = 4 * HEAD_DIM


def _inproj1_kernel(x_ref, sh_ref, sc_ref, w_ref, gain_ref, ind_ref, indt_ref, cos_ref, sin_ref, *out_refs,
                    n_norm_chunks, out_layout, rope):
    h = (x_ref[0] * (1.0 + sc_ref[0]) + sh_ref[0]).astype(BF16)
    lane_hi = (lax.broadcasted_iota(jnp.int32, (1, LANES), 1) % HEAD_DIM) >= HEAD_DIM // 2
    targets = iter([(ref, i, dup) for ref, (n, dup) in zip(out_refs, out_layout)
                    for i in range(0, n, 2 if dup else 1)])
    n_chunks = w_ref.shape[1] // NORM_CHUNK
    for c in range(n_chunks):
        cols = slice(c * NORM_CHUNK, (c + 1) * NORM_CHUNK)
        p = _dot(h, w_ref[:, cols])
        if c < n_norm_chunks:
            ms = _dot2(p * p, ind_ref[...]) * (1.0 / HEAD_DIM)
            r = _dot2(lax.rsqrt(ms + RMS_EPS), indt_ref[...])
            p = p * r * gain_ref[:, cols]
            if rope:
                p = jnp.concatenate(
                    [_rope128(p[:, j * LANES:(j + 1) * LANES], cos_ref[...], sin_ref[...], lane_hi)
                     for j in range(NORM_CHUNK // LANES)], axis=1)
        _store_pairs(p, targets)


def _inproj1(x, shift, scale, w, gain, cos, sin, *, tm, out_layout, n_norm_chunks, rope):
    bsz, n_tok, _ = x.shape
    per_batch = shift.shape[0] > 1
    mod_map = (lambda b, i: (b, 0, 0)) if per_batch else (lambda b, i: (0, 0, 0))
    ind_np = np.zeros((NORM_CHUNK, LANES), np.float32)
    ind_np[np.arange(NORM_CHUNK), np.arange(NORM_CHUNK) // HEAD_DIM] = 1.0
    ind = jnp.asarray(ind_np, BF16)
    indt = jnp.asarray(ind_np.T, BF16)
    const = lambda a: pl.BlockSpec(a.shape, lambda b, i: (0,) * a.ndim)
    return pl.pallas_call(
        functools.partial(_inproj1_kernel, n_norm_chunks=n_norm_chunks, out_layout=out_layout, rope=rope),
        grid=(bsz, n_tok // tm),
        in_specs=[pl.BlockSpec((1, tm, D_MODEL), lambda b, i: (b, i, 0)),
                  pl.BlockSpec((1, 1, D_MODEL), mod_map),
                  pl.BlockSpec((1, 1, D_MODEL), mod_map),
                  const(w), const(gain), const(ind), const(indt),
                  pl.BlockSpec((tm, LANES), lambda b, i: (i, 0)),
                  pl.BlockSpec((tm, LANES), lambda b, i: (i, 0))],
        out_specs=[pl.BlockSpec((1, n, tm, LANES), lambda b, i: (b, 0, i, 0)) for n, _ in out_layout],
        out_shape=[jax.ShapeDtypeStruct((bsz, n, n_tok, LANES), BF16) for n, _ in out_layout],
        compiler_params=_cparams(("arbitrary", "arbitrary")),
        name="inproj1",
    )(x, shift, scale, w, gain, ind, indt, cos, sin)


def _gqa_kernel(q_ref, k_ref, kc_ref, v_ref, vc_ref, o_ref):
    for p in range(GQA_Q_HEADS // 2):
        g = 2 * p // GQA_GROUP
        ks, vs = [k_ref[0, g], kc_ref[0, g]], [v_ref[0, g], vc_ref[0, g]]
        outs = [_softmax_attend(q, ks, vs, [None, None]) for q in _head_halves(q_ref[0, p])]
        o_ref[0, :, p * LANES:(p + 1) * LANES] = _merge_halves(*outs).astype(BF16)


def _gqa_attention(q, k, v, k_c, v_c, *, tq):
    bsz, n_p, n_tok, _ = q.shape
    full = lambda a: pl.BlockSpec((1,) + a.shape[1:], lambda b, i: (b, 0, 0, 0))
    return pl.pallas_call(
        _gqa_kernel,
        grid=(bsz, n_tok // tq),
        in_specs=[pl.BlockSpec((1, n_p, tq, LANES), lambda b, i: (b, 0, i, 0)),
                  full(k), full(k_c), full(v), full(v_c)],
        out_specs=pl.BlockSpec((1, tq, n_p * LANES), lambda b, i: (b, i, 0)),
        out_shape=jax.ShapeDtypeStruct((bsz, n_tok, n_p * LANES), BF16),
        compiler_params=_cparams(("arbitrary", "arbitrary")),
        name="gqa_attention",
    )(q, k, k_c, v, v_c)


def _out1_kernel(x_ref, y_ref, g1_ref, sh2_ref, sc2_ref, wo_ref, lng_ref, lnb_ref, rw_ref,
                 x1_ref, hm_ref, route_ref):
    rw_hi, rw_lo = _split(rw_ref[...])
    n_sub = 2
    rows_sub = x_ref.shape[1] // n_sub
    for sub in range(n_sub):
        rows = slice(sub * rows_sub, (sub + 1) * rows_sub)
        y = _dot(y_ref[0, rows, :], wo_ref[...])
        x1 = _layer_norm(DEEPNORM_ALPHA * x_ref[0, rows, :] + g1_ref[0] * y, lng_ref[...], lnb_ref[...])
        x1_ref[0, rows, :] = x1
        hm = x1 * (1.0 + sc2_ref[0]) + sh2_ref[0]
        _to_row_tiles(hm, hm_ref.at[0, pl.ds(sub * rows_sub * ROW_TILE, rows_sub * ROW_TILE)])
        lane = lax.broadcasted_iota(jnp.int32, (rows_sub, LANES), 1)
        hm_hi, hm_lo = _split(hm)
        logits = _dot(hm_hi, rw_hi) + (_dot(hm_hi, rw_lo) + _dot(hm_lo, rw_hi))
        logits = jnp.where(lane < N_EXPERTS, logits, NEG_INF)
        v1 = logits.max(axis=-1, keepdims=True)
        i1 = jnp.where(logits == v1, lane, LANES).min(axis=-1, keepdims=True)
        rest = jnp.where(lane == i1, NEG_INF, logits)
        v2 = rest.max(axis=-1, keepdims=True)
        i2 = jnp.where(rest == v2, lane, LANES).min(axis=-1, keepdims=True)
        e = jnp.exp(v2 - v1)
        inv = 1.0 / (1.0 + e)
        route_ref[0, rows, :] = jnp.where(
            lane == 0, i1.astype(F32),
            jnp.where(lane == 1, i2.astype(F32), jnp.where(lane == 2, inv, jnp.where(lane == 3, e * inv, 0.0))))


def _out1(x, y, g1, sh2, sc2, wo, lng, lnb, rw, *, tm):
    bsz, n_tok, _ = x.shape
    mod = pl.BlockSpec((1, 1, D_MODEL), lambda b, i: (b, 0, 0))
    tile = pl.BlockSpec((1, tm, D_MODEL), lambda b, i: (b, i, 0))
    const = lambda a: pl.BlockSpec(a.shape, lambda b, i: (0,) * a.ndim)
    vec = lambda a: a.reshape(1, D_MODEL)
    return pl.pallas_call(
        _out1_kernel,
        grid=(bsz, n_tok // tm),
        in_specs=[tile, tile, mod, mod, mod, const(wo), const(vec(lng)), const(vec(lnb)), const(rw)],
        out_specs=[tile, pl.BlockSpec((1, tm * ROW_TILE, LANES), lambda b, i: (b, i, 0)),
                   pl.BlockSpec((1, tm, LANES), lambda b, i: (b, i, 0))],
        out_shape=[jax.ShapeDtypeStruct(x.shape, F32), jax.ShapeDtypeStruct((bsz, n_tok * ROW_TILE, LANES), F32),
                   jax.ShapeDtypeStruct((bsz, n_tok, LANES), F32)],
        compiler_params=_cparams(("arbitrary", "arbitrary")),
        name="out1",
    )(x, y, g1, sh2, sc2, wo, vec(lng), vec(lnb), rw)


def _moe_kernel(be_ref, nv_ref, gfirst_ref, gnext_ref, sprev_ref, slast_ref, hm_ref, w1_ref, w3_ref, w2_ref,
                y_ref, xg_ref, xb_ref, hid_ref, acc_ref, out_ref, gsem, ssem):
    del be_ref
    i, j = pl.program_id(0), pl.program_id(1)
    n_valid = nv_ref[0]
    slot = i % 2
    other = 1 - slot
    tile_rows = lambda row: pl.ds(pl.multiple_of(row * ROW_TILE, ROW_TILE), ROW_TILE)

    def gather_row(idx_ref, row, dst_slot):
        return pltpu.make_async_copy(hm_ref.at[tile_rows(idx_ref[0, 0, row])],
                                     xg_ref.at[dst_slot, tile_rows(row)], gsem.at[dst_slot])

    def scatter_row(idx_ref, row, src_slot):
        return pltpu.make_async_copy(out_ref.at[src_slot, tile_rows(row)],
                                     y_ref.at[tile_rows(idx_ref[0, 0, row])], ssem.at[src_slot])

    def wait_gather(s):
        pltpu.make_async_copy(hm_ref.at[pl.ds(0, MOE_TM * ROW_TILE)], xg_ref.at[s], gsem.at[s]).wait()

    def wait_scatter(s):
        pltpu.make_async_copy(out_ref.at[s], y_ref.at[pl.ds(0, MOE_TM * ROW_TILE)], ssem.at[s]).wait()

    @pl.when(jnp.logical_and(i == 0, j == 0))
    def _():
        out_ref[1] = jnp.zeros(out_ref.shape[1:], F32)

        def start(r, carry):
            gather_row(gfirst_ref, r, 0).start()
            return carry

        lax.fori_loop(0, MOE_TM, start, 0)
        n_real = y_ref.shape[0] - 2 * MOE_TM * ROW_TILE
        fill = pltpu.make_async_copy(out_ref.at[1], y_ref.at[pl.ds(n_real, MOE_TM * ROW_TILE)], ssem.at[1])
        fill.start()
        fill.wait()

    @pl.when(jnp.logical_and(i < n_valid, j == 0))
    def _():
        wait_gather(slot)
        xb_ref[...] = _from_row_tiles(xg_ref.at[slot], MOE_TM).astype(BF16)

        @pl.when(i >= 1)
        def _():
            wait_scatter(slot)

        acc_ref[...] = jnp.zeros(acc_ref.shape, F32)

    @pl.when(i < n_valid)
    def _():
        base = j * MOE_ROWS_PER_STEP
        n_up = MOE_TF // MOE_CHUNK
        n_down = D_MODEL // MOE_CHUNK
        per_group = -(-MOE_ROWS_PER_STEP // (n_up + n_down))
        rows = iter(range(MOE_ROWS_PER_STEP))

        def copy_group():
            for _ in range(per_group):
                r = next(rows, None)
                if r is not None:
                    gather_row(gnext_ref, base + r, other).start(priority=1)
                    scatter_row(sprev_ref, base + r, other).start(priority=r % 2)

        cols = lambda n: slice(n * MOE_CHUNK, (n + 1) * MOE_CHUNK)
        up = lambda n: (_dot(xb_ref[...], w1_ref[0, :, cols(n)]), _dot(xb_ref[...], w3_ref[0, :, cols(n)]))
        pending = up(0)
        for n in range(n_up):
            following = up(n + 1) if n + 1 < n_up else None
            a, g = pending
            hid_ref[:, cols(n)] = (_silu(a) * g).astype(BF16)
            copy_group()
            pending = following
        down = lambda n: _dot(hid_ref[...], w2_ref[0, :, cols(n)])
        pending = down(0)
        for n in range(n_down):
            following = down(n + 1) if n + 1 < n_down else None
            acc_ref[:, cols(n)] += pending
            copy_group()
            pending = following

        @pl.when(j == MOE_NF - 1)
        def _():
            _to_row_tiles(acc_ref[...], out_ref.at[slot])

    @pl.when(jnp.logical_and(i == n_valid - 1, j == MOE_NF - 1))
    def _():
        def start(r, carry):
            scatter_row(slast_ref, r, slot).start()
            return carry

        lax.fori_loop(0, MOE_TM, start, 0)
        wait_gather(other)
        wait_scatter(other)
        wait_scatter(slot)


def _moe_experts(hm_packed, plan, w13, w2):
    block_e, n_valid, gidx, sidx, n_out_rows = plan
    n_blocks = block_e.shape[0]
    smem_rows = lambda index_map: pl.BlockSpec((1, 1, MOE_TM), index_map, memory_space=pltpu.SMEM)
    return pl.pallas_call(
        _moe_kernel,
        grid_spec=pltpu.PrefetchScalarGridSpec(
            num_scalar_prefetch=2,
            grid=(n_blocks, MOE_NF),
            in_specs=[smem_rows(lambda i, j, be, nv: (0, 0, 0)),
                      smem_rows(lambda i, j, be, nv: (i + 1, 0, 0)),
                      smem_rows(lambda i, j, be, nv: (i, 0, 0)),
                      smem_rows(lambda i, j, be, nv: (nv[0], 0, 0)),
                      pl.BlockSpec(memory_space=pl.ANY),
                      pl.BlockSpec((1, D_MODEL, MOE_TF), lambda i, j, be, nv: (be[i], 0, j)),
                      pl.BlockSpec((1, D_MODEL, MOE_TF), lambda i, j, be, nv: (be[i], 0, MOE_NF + j)),
                      pl.BlockSpec((1, MOE_TF, D_MODEL), lambda i, j, be, nv: (be[i], j, 0))],
            out_specs=pl.BlockSpec(memory_space=pl.ANY),
            scratch_shapes=[pltpu.VMEM((2, MOE_TM * ROW_TILE, LANES), F32),
                            pltpu.VMEM((MOE_TM, D_MODEL), BF16),
                            pltpu.VMEM((MOE_TM, MOE_TF), BF16),
                            pltpu.VMEM((MOE_TM, D_MODEL), F32),
                            pltpu.VMEM((2, MOE_TM * ROW_TILE, LANES), F32),
                            pltpu.SemaphoreType.DMA((2,)),
                            pltpu.SemaphoreType.DMA((2,))]),
        out_shape=jax.ShapeDtypeStruct((n_out_rows * ROW_TILE, LANES), F32),
        compiler_params=_cparams(("arbitrary", "arbitrary")),
        name="moe_experts",
    )(block_e, n_valid, gidx, gidx, sidx, sidx, hm_packed, w13, w13, w2)


def _combine_kernel(x1_ref, y0_ref, y1_ref, route_ref, g2_ref, lng_ref, lnb_ref, o_ref):
    route = route_ref[...]
    n = x1_ref.shape[0]
    m = _from_row_tiles(y0_ref, n) * route[:, 2:3] + _from_row_tiles(y1_ref, n) * route[:, 3:4]
    o_ref[...] = _layer_norm(DEEPNORM_ALPHA * x1_ref[...] + g2_ref[0] * m, lng_ref[...], lnb_ref[...])


def _combine(x1, y_rows, route, g2, lng, lnb, *, tm, tiles_per_batch):
    n_tok = x1.shape[0]
    y_tile = lambda first: pl.BlockSpec((tm * ROW_TILE, LANES), lambda i: (i + first // tm, 0))
    vec = lambda a: a.reshape(1, D_MODEL)
    const = lambda a: pl.BlockSpec(a.shape, lambda i: (0,) * a.ndim)
    return pl.pallas_call(
        _combine_kernel,
        grid=(n_tok // tm,),
        in_specs=[pl.BlockSpec((tm, D_MODEL), lambda i: (i, 0)),
                  y_tile(0), y_tile(n_tok),
                  pl.BlockSpec((tm, LANES), lambda i: (i, 0)),
                  pl.BlockSpec((1, 1, D_MODEL), lambda i: (i // tiles_per_batch, 0, 0)),
                  const(vec(lng)), const(vec(lnb))],
        out_specs=pl.BlockSpec((tm, D_MODEL), lambda i: (i, 0)),
        out_shape=jax.ShapeDtypeStruct(x1.shape, F32),
        compiler_params=_cparams(("arbitrary",)),
        name="moe_combine",
    )(x1, y_rows, y_rows, route, g2, vec(lng), vec(lnb))


def _routing_plan(route, n_tok):
    n_pairs = n_tok * TOP_K
    flat_e = route[:, :TOP_K].astype(jnp.int32).reshape(-1)
    pair = jnp.arange(n_pairs, dtype=jnp.int32)
    sorted_key = jnp.sort(flat_e * n_pairs + pair)
    order = sorted_key - (sorted_key // n_pairs) * n_pairs
    experts = jnp.arange(N_EXPERTS, dtype=jnp.int32)
    counts = jnp.sum((flat_e[:, None] == experts[None, :]).astype(jnp.int32), axis=0)
    first_pair = jnp.cumsum(counts) - counts
    blocks_per = (counts + MOE_TM - 1) // MOE_TM
    block_end = jnp.cumsum(blocks_per)
    block_start = block_end - blocks_per
    n_blocks = -(-(n_pairs + N_EXPERTS * (MOE_TM - 1)) // MOE_TM)
    blk = jnp.arange(n_blocks, dtype=jnp.int32)
    block_e = jnp.minimum(jnp.sum((block_end[None, :] <= blk[:, None]).astype(jnp.int32), axis=1), N_EXPERTS - 1)
    n_valid = block_end[-1]
    sel = (block_e[:, None] == experts[None, :]).astype(jnp.int32)
    pick = lambda v: jnp.sum(sel * v[None, :], axis=1)
    offset = (blk - pick(block_start)) * MOE_TM
    src = jnp.clip(pick(first_pair) + offset, 0, n_pairs)
    r = jnp.arange(MOE_TM, dtype=jnp.int32)[None, :]
    pairs = jnp.take(order, jnp.minimum(src[:, None] + r, n_pairs - 1), axis=0)
    used = jnp.logical_and((blk < n_valid)[:, None], offset[:, None] + r < pick(counts)[:, None])
    spare = n_pairs + (blk % 2)[:, None] * MOE_TM + r
    token, choice = pairs // TOP_K, pairs % TOP_K
    gidx = jnp.where(used, token, 0)
    sidx = jnp.where(used, choice * n_tok + token, spare)
    gidx = jnp.concatenate([gidx, jnp.zeros((1, MOE_TM), jnp.int32)], axis=0)
    sidx = jnp.concatenate([n_pairs + MOE_TM + r, sidx], axis=0)
    shape3 = (n_blocks + 1, 1, MOE_TM)
    return (block_e.astype(jnp.int32), n_valid.astype(jnp.int32).reshape(1), gidx.reshape(shape3),
            sidx.reshape(shape3), n_pairs + 2 * MOE_TM)


def _rope_tables(n_tok):
    t = jnp.arange(n_tok, dtype=jnp.int32)
    n_freq = HEAD_DIM // 4
    inv_freq = ROPE_THETA ** (-jnp.arange(n_freq, dtype=F32) / n_freq)
    ang = jnp.concatenate([(t // GRID_W).astype(F32)[:, None] * inv_freq,
                           (t % GRID_W).astype(F32)[:, None] * inv_freq], axis=-1)
    cos, sin = jnp.cos(ang), jnp.sin(ang)
    cos128 = jnp.tile(cos, (1, 4))
    sin128 = jnp.tile(jnp.concatenate([-sin, sin], axis=-1), (1, 2))
    return cos128, sin128


def _mods(cond_rows, ada_w, ada_b, bsz):
    out = _ada_mods(cond_rows, ada_w, ada_b)
    lat = [m.reshape(bsz, 1, D_MODEL) for m in jnp.split(out[:bsz], 6, axis=-1)]
    ctx = [m.reshape(1, 1, D_MODEL) for m in jnp.split(out[bsz:bsz + 1], 6, axis=-1)]
    return lat, ctx


def kernel(x, c, ctx, c_ctx, l0_ada_w, l0_ada_b, l0_w_in, l0_rpb, l0_lambda_qk, l0_subln_g, l0_w_out, l0_ln1_g, l0_ln1_b, l0_ffn_w13, l0_ffn_w2, l0_ln2_g, l0_ln2_b, l1_ada_w, l1_ada_b, l1_w_in, l1_q_norm_g, l1_k_norm_g, l1_w_out, l1_ln1_g, l1_ln1_b, l1_router_w, l1_moe_w13, l1_moe_w2, l1_ln2_g, l1_ln2_b):
    bsz, n_tok, _ = x.shape
    n_ctx = ctx.shape[1]
    assert n_tok == GRID_ROWS * GRID_W and n_tok % TOKEN_TILE == 0 and n_tok % ATTN_TQ == 0
    assert n_ctx % NA_KEY_CHUNK == 0 and bsz < COND_ROWS
    cond_rows = jnp.concatenate([c, c_ctx[None, :], jnp.zeros((COND_ROWS - bsz - 1, D_MODEL), F32)], axis=0)
    cos, sin = _rope_tables(n_tok)
    ones = jnp.ones((n_ctx, LANES), F32)
    zeros = jnp.zeros((n_ctx, LANES), F32)

    (sh1, sc1, g1, sh2, sc2, g2), (csh1, csc1, cg1, csh2, csc2, cg2) = _mods(cond_rows, l0_ada_w, l0_ada_b, bsz)
    w_in0 = l0_w_in.astype(BF16)
    qa, ka, va, qb, kb, vb = _inproj0(x, sh1, sc1, w_in0, cos, sin, tm=TOKEN_TILE, rope=True)
    qa_c, ka_c, va_c, qb_c, kb_c, vb_c = _inproj0(ctx, csh1, csc1, w_in0, ones, zeros, tm=n_ctx, rope=False)
    lam_init = 0.8 - 0.6 * math.exp(-0.3 * 0)
    lq = l0_lambda_qk.astype(F32)
    lam = (jnp.exp(jnp.sum(lq[0] * lq[1])) - jnp.exp(jnp.sum(lq[2] * lq[3])) + lam_init).reshape(1, 1)
    y_a = _na_attention(qa, ka, va, ka_c, va_c, _na_bias_table(l0_rpb))
    y_b = _diff_attention(lam, qb, kb, vb, kb_c, vb_c, l0_subln_g, tq=ATTN_TQ, out_scale=1.0 - lam_init)
    ya_c, yb_c = _ctx_attention0(lam, qa_c, ka_c, va_c, qb_c, kb_c, vb_c, l0_subln_g, out_scale=1.0 - lam_init)
    wo0, w13_0, w2_0 = l0_w_out.astype(BF16), l0_ffn_w13.astype(BF16), l0_ffn_w2.astype(BF16)
    ln0 = (l0_ln1_g, l0_ln1_b, l0_ln2_g, l0_ln2_b)
    x = _post0(x, y_a, y_b, g1, sh2, sc2, g2, wo0, w13_0, w2_0, *ln0, tm=TOKEN_TILE)
    ctx = _post0(ctx, ya_c, yb_c, cg1, csh2, csc2, cg2, wo0, w13_0, w2_0, *ln0, tm=n_ctx)

    (sh1, sc1, g1, sh2, sc2, g2), (csh1, csc1, _, _, _, _) = _mods(cond_rows, l1_ada_w, l1_ada_b, bsz)
    w_in1 = l1_w_in.astype(BF16)
    n_q, n_kv = GQA_Q_HEADS * HEAD_DIM, GQA_KV_HEADS * HEAD_DIM
    gain = jnp.concatenate([jnp.tile(l1_q_norm_g, GQA_Q_HEADS) * QK_SCALE,
                            jnp.tile(l1_k_norm_g, GQA_KV_HEADS)]).reshape(1, n_q + n_kv)
    q, k, v = _inproj1(x, sh1, sc1, w_in1, gain, cos, sin, tm=TOKEN_TILE,
                       out_layout=((GQA_Q_HEADS // 2, False), (GQA_KV_HEADS, True), (GQA_KV_HEADS, True)),
                       n_norm_chunks=(n_q + n_kv) // NORM_CHUNK, rope=True)
    k_c, v_c = _inproj1(ctx, csh1, csc1, w_in1[:, n_q:], gain[:, n_q:], ones, zeros, tm=n_ctx,
                        out_layout=((GQA_KV_HEADS, True), (GQA_KV_HEADS, True)),
                        n_norm_chunks=n_kv // NORM_CHUNK, rope=False)
    y = _gqa_attention(q, k, v, k_c, v_c, tq=ATTN_TQ)
    rw = jnp.pad(l1_router_w.astype(F32), ((0, 0), (0, LANES - N_EXPERTS)))
    x1, hm, route = _out1(x, y, g1, sh2, sc2, l1_w_out.astype(BF16), l1_ln1_g, l1_ln1_b, rw, tm=TOKEN_TILE)
    n_all = bsz * n_tok
    x1, hm, route = x1.reshape(n_all, D_MODEL), hm.reshape(n_all * ROW_TILE, LANES), route.reshape(n_all, LANES)
    y_rows = _moe_experts(hm, _routing_plan(route, n_all), l1_moe_w13.astype(BF16), l1_moe_w2.astype(BF16))
    out = _combine(x1, y_rows, route, g2, l1_ln2_g, l1_ln2_b, tm=TOKEN_TILE, tiles_per_batch=n_tok // TOKEN_TILE)
    return out.reshape(bsz, n_tok, D_MODEL)
```

```python
import functools
import math

import jax
import jax.numpy as jnp
import numpy as np
from jax import lax
from jax.experimental import pallas as pl
from jax.experimental.pallas import tpu as pltpu

F32 = jnp.float32
BF16 = jnp.bfloat16

D_MODEL = 1024
DEPTH = 2
GRID_W = 64
GRID_ROWS = 32
HEAD_DIM = 64
ROPE_THETA = 10000.0
NA_HEADS = 8
NA_WIN_H = 8
NA_WIN_W = 16
DIFF_HEADS = 4
DIFF_VDIM = 2 * HEAD_DIM
GQA_Q_HEADS = 16
GQA_KV_HEADS = 4
GQA_GROUP = GQA_Q_HEADS // GQA_KV_HEADS
FFN_DIM = 2816
N_EXPERTS = 8
TOP_K = 2
EXPERT_DIM = 3584
DEEPNORM_ALPHA = (2 * DEPTH) ** 0.25
LN_EPS = 1e-5
RMS_EPS = 1e-6
NEG_INF = -1e30
LOG2E = math.log2(math.e)
QK_SCALE = HEAD_DIM ** -0.5 * LOG2E

LANES = 128
VMEM_LIMIT = 56 * 1024 * 1024

NA_TILE_ROWS = 4
NA_KEY_ROWS = 12
NA_TQ = NA_TILE_ROWS * GRID_W
NA_KEY_CHUNK = NA_TQ
NA_N_KEY_CHUNKS = NA_KEY_ROWS * GRID_W // NA_KEY_CHUNK

TOKEN_TILE = 512
ATTN_TQ = 256
COND_ROWS = 32

MOE_TM = 1024
MOE_TF = 1792
MOE_CHUNK = 256
MOE_NF = EXPERT_DIM // MOE_TF
MOE_ROWS_PER_STEP = MOE_TM // MOE_NF


def _cparams(sem, vmem=VMEM_LIMIT):
    return pltpu.CompilerParams(dimension_semantics=sem, vmem_limit_bytes=vmem)


def _dot(a, b):
    return jnp.dot(a, b, preferred_element_type=F32)


def _dot_t(a, b):
    return lax.dot_general(a, b, (((1,), (1,)), ((), ())), preferred_element_type=F32)


def _split(a):
    hi = a.astype(BF16)
    lo = (a - hi.astype(F32)).astype(BF16)
    return hi, lo


def _dot3(a, b):
    a_hi, a_lo = _split(a)
    b_hi, b_lo = _split(b)
    return _dot(a_hi, b_hi) + (_dot(a_hi, b_lo) + _dot(a_lo, b_hi))


def _dot2(a, b_exact):
    a_hi, a_lo = _split(a)
    return _dot(a_hi, b_exact) + _dot(a_lo, b_exact)


def _layer_norm(z, g, b):
    mu = jnp.mean(z, axis=-1, keepdims=True)
    zc = z - mu
    var = jnp.mean(zc * zc, axis=-1, keepdims=True)
    return zc * lax.rsqrt(var + LN_EPS) * g + b


def _silu(a):
    return a * jax.nn.sigmoid(a)


def _rope128(y, cos, sin_signed, lane_hi):
    swapped = jnp.where(lane_hi, pltpu.roll(y, 32, 1), pltpu.roll(y, 96, 1))
    return y * cos + swapped * sin_signed


ROW_TILE = 8
assert D_MODEL == ROW_TILE * LANES


def _to_row_tiles(x, ref):
    for s in range(ROW_TILE):
        ref[pl.ds(s, x.shape[0], stride=ROW_TILE), :] = x[:, s * LANES:(s + 1) * LANES]


def _from_row_tiles(ref, n):
    return jnp.concatenate([ref[pl.ds(s, n, stride=ROW_TILE), :] for s in range(ROW_TILE)], axis=1)


def _head_halves(pair):
    lane = lax.broadcasted_iota(jnp.int32, pair.shape, 1)
    zero = jnp.zeros_like(pair)
    return jnp.where(lane < HEAD_DIM, pair, zero), jnp.where(lane >= HEAD_DIM, pair, zero)


def _merge_halves(lo, hi):
    lane = lax.broadcasted_iota(jnp.int32, lo.shape, 1)
    return jnp.where(lane < HEAD_DIM, lo, hi)


def _store_pairs(y, refs_iter):
    lane = lax.broadcasted_iota(jnp.int32, (1, LANES), 1)
    for c in range(y.shape[1] // LANES):
        slab = y[:, c * LANES:(c + 1) * LANES]
        ref, idx, dup = next(refs_iter)
        if dup:
            rolled = pltpu.roll(slab, HEAD_DIM, 1)
            ref[0, idx] = jnp.where(lane < HEAD_DIM, slab, rolled).astype(BF16)
            ref[0, idx + 1] = jnp.where(lane < HEAD_DIM, rolled, slab).astype(BF16)
        else:
            ref[0, idx] = slab.astype(BF16)


def _ada_kernel(c_ref, w_ref, b_ref, o_ref):
    o_ref[...] = _dot3(_silu(c_ref[...]), w_ref[...]) + b_ref[...]


def _ada_mods(cond, w, b):
    rows, tn = cond.shape[0], 1536
    n_out = w.shape[1]
    return pl.pallas_call(
        _ada_kernel,
        grid=(n_out // tn,),
        in_specs=[pl.BlockSpec((rows, D_MODEL), lambda j: (0, 0)),
                  pl.BlockSpec((D_MODEL, tn), lambda j: (0, j)),
                  pl.BlockSpec((1, tn), lambda j: (0, j))],
        out_specs=pl.BlockSpec((rows, tn), lambda j: (0, j)),
        out_shape=jax.ShapeDtypeStruct((rows, n_out), F32),
        compiler_params=_cparams(("arbitrary",)),
        name="ada_mods",
    )(cond, w, b.reshape(1, n_out))


def _inproj0_kernel(x_ref, sh_ref, sc_ref, w_ref, cos_ref, sin_ref,
                    qa_ref, ka_ref, va_ref, qb_ref, kb_ref, vb_ref, *, rope):
    h = (x_ref[0] * (1.0 + sc_ref[0]) + sh_ref[0]).astype(BF16)
    lane_hi = (lax.broadcasted_iota(jnp.int32, (1, LANES), 1) % HEAD_DIM) >= HEAD_DIM // 2
    sec = NA_HEADS * HEAD_DIM
    plan = ((qa_ref, QK_SCALE, False), (ka_ref, None, False), (va_ref, None, False),
            (qb_ref, QK_SCALE, True), (kb_ref, None, True))
    for s, (ref, scale, rotary) in enumerate(plan):
        p = _dot(h, w_ref[:, s * sec:(s + 1) * sec])
        if scale is not None:
            p = p * scale
        for c in range(sec // LANES):
            y = p[:, c * LANES:(c + 1) * LANES]
            if rotary and rope:
                y = _rope128(y, cos_ref[...], sin_ref[...], lane_hi)
            ref[0, c] = y.astype(BF16)
    p = _dot(h, w_ref[:, 5 * sec:6 * sec])
    for hh in range(DIFF_HEADS):
        vb_ref[0, hh] = p[:, hh * DIFF_VDIM:(hh + 1) * DIFF_VDIM].astype(BF16)


def _inproj0(x, shift, scale, w, cos, sin, *, tm, rope):
    bsz, n_tok, _ = x.shape
    per_batch = shift.shape[0] > 1
    mod_map = (lambda b, i: (b, 0, 0)) if per_batch else (lambda b, i: (0, 0, 0))
    n_pairs = NA_HEADS // 2
    out_shape = jax.ShapeDtypeStruct((bsz, n_pairs, n_tok, LANES), BF16)
    out_spec = pl.BlockSpec((1, n_pairs, tm, LANES), lambda b, i: (b, 0, i, 0))
    return pl.pallas_call(
        functools.partial(_inproj0_kernel, rope=rope),
        grid=(bsz, n_tok // tm),
        in_specs=[pl.BlockSpec((1, tm, D_MODEL), lambda b, i: (b, i, 0)),
                  pl.BlockSpec((1, 1, D_MODEL), mod_map),
                  pl.BlockSpec((1, 1, D_MODEL), mod_map),
                  pl.BlockSpec(w.shape, lambda b, i: (0, 0)),
                  pl.BlockSpec((tm, LANES), lambda b, i: (i, 0)),
                  pl.BlockSpec((tm, LANES), lambda b, i: (i, 0))],
        out_specs=[out_spec] * 6,
        out_shape=[out_shape] * 6,
        compiler_params=_cparams(("arbitrary", "arbitrary")),
        name="inproj0",
    )(x, shift, scale, w, cos, sin)


def _scores(q, ks, biases):
    out = []
    for k, bias in zip(ks, biases):
        s = _dot_t(q, k)
        out.append(s if bias is None else s + bias)
    return out


def _softmax_parts(ss):
    m = ss[0].max(axis=-1, keepdims=True)
    for s in ss[1:]:
        m = jnp.maximum(m, s.max(axis=-1, keepdims=True))
    es = [jnp.exp2(s - m) for s in ss]
    l = es[0].sum(axis=-1, keepdims=True)
    for e in es[1:]:
        l = l + e.sum(axis=-1, keepdims=True)
    return es, l


def _softmax_attend(q, ks, vs, biases):
    es, l = _softmax_parts(_scores(q, ks, biases))
    o = _dot(es[0].astype(BF16), vs[0])
    for e, v in zip(es[1:], vs[1:]):
        o = o + _dot(e.astype(BF16), v)
    return o * (1.0 / l)


def _diff_attend(q0, q1, k0s, k1s, vs, lam, subln_g, out_scale):
    e0, l0 = _softmax_parts(_scores(q0, k0s, [None] * len(k0s)))
    e1, l1 = _softmax_parts(_scores(q1, k1s, [None] * len(k1s)))
    c = lam * l0 / l1
    y = None
    for a0, a1, v in zip(e0, e1, vs):
        t = _dot((a0 - a1 * c).astype(BF16), v)
        y = t if y is None else y + t
    y = y * (1.0 / l0)
    y = y * lax.rsqrt(jnp.mean(y * y, axis=-1, keepdims=True) + RMS_EPS) * subln_g
    return y * out_scale


def _na_kernel(q_ref, k0_ref, k1_ref, k2_ref, v0_ref, v1_ref, v2_ref, kc_ref, vc_ref, bias_ref,
               o_ref):
    for p in range(NA_HEADS // 2):
        ks = [k0_ref[0, p], k1_ref[0, p], k2_ref[0, p], kc_ref[0, p]]
        vs = [v0_ref[0, p], v1_ref[0, p], v2_ref[0, p], vc_ref[0, p]]
        outs = []
        for half, q in enumerate(_head_halves(q_ref[0, p])):
            h = 2 * p + half
            biases = [bias_ref[0, h, :, j * NA_KEY_CHUNK:(j + 1) * NA_KEY_CHUNK] for j in range(NA_N_KEY_CHUNKS)]
            outs.append(_softmax_attend(q, ks, vs, biases + [None]))
        o_ref[0, :, p * LANES:(p + 1) * LANES] = _merge_halves(*outs).astype(BF16)


def _na_bias_table(rpb):
    rows_total = GRID_ROWS
    n_h = rpb.shape[0]
    pad = GRID_W - NA_WIN_W
    rpb_pad = jnp.pad(rpb.astype(F32) * LOG2E, ((0, 0), (0, 0), (pad, pad)))
    col_blocks = jnp.stack([rpb_pad[:, :, pad + NA_WIN_W - 1 - qc:pad + NA_WIN_W - 1 - qc + GRID_W]
                            for qc in range(GRID_W)], axis=2)
    q_c = np.arange(GRID_W)
    col_start = np.clip(q_c - NA_WIN_W // 2, 0, GRID_W - NA_WIN_W)
    col_ok = (q_c[None, :] >= col_start[:, None]) & (q_c[None, :] < col_start[:, None] + NA_WIN_W)
    col_blocks = jnp.where(jnp.asarray(col_ok)[None, None], col_blocks, NEG_INF)
    masked = jnp.full((n_h, GRID_W, GRID_W), NEG_INF, F32)
    kinds = []
    for r0 in (0, NA_TILE_ROWS, GRID_ROWS - 2 * NA_TILE_ROWS, GRID_ROWS - NA_TILE_ROWS):
        ws = int(np.clip(r0 - NA_WIN_H // 2, 0, rows_total - NA_KEY_ROWS))
        q_rows = []
        for qr in range(r0, r0 + NA_TILE_ROWS):
            row_start = int(np.clip(qr - NA_WIN_H // 2, 0, rows_total - NA_WIN_H))
            blocks = []
            for kr in range(ws, ws + NA_KEY_ROWS):
                inside = row_start <= kr < row_start + NA_WIN_H
                blocks.append(col_blocks[:, kr - qr + NA_WIN_H - 1] if inside else masked)
            q_rows.append(jnp.concatenate(blocks, axis=-1))
        kinds.append(jnp.concatenate(q_rows, axis=1))
    return jnp.stack(kinds, axis=0)


def _na_attention(qa, ka, va, ka_c, va_c, bias_tab):
    bsz, n_p, n_tok, _ = qa.shape
    n_ctx = ka_c.shape[2]
    n_tiles = n_tok // NA_TQ
    last_start = (n_tok - NA_KEY_ROWS * GRID_W) // NA_KEY_CHUNK

    def win(j):
        return lambda t, b: (b, 0, jnp.clip(t - 1, 0, last_start) + j, 0)

    kv_spec = lambda j: pl.BlockSpec((1, n_p, NA_KEY_CHUNK, LANES), win(j))
    ctx_spec = pl.BlockSpec((1, n_p, n_ctx, LANES), lambda t, b: (b, 0, 0, 0))
    bias_map = lambda t, b: (jnp.minimum(t, 1) + jnp.maximum(t - (n_tiles - 3), 0), 0, 0, 0)
    return pl.pallas_call(
        _na_kernel,
        grid=(n_tiles, bsz),
        in_specs=[pl.BlockSpec((1, n_p, NA_TQ, LANES), lambda t, b: (b, 0, t, 0)),
                  kv_spec(0), kv_spec(1), kv_spec(2), kv_spec(0), kv_spec(1), kv_spec(2),
                  ctx_spec, ctx_spec,
                  pl.BlockSpec((1, NA_HEADS, NA_TQ, NA_KEY_ROWS * GRID_W), bias_map)],
        out_specs=pl.BlockSpec((1, NA_TQ, n_p * LANES), lambda t, b: (b, t, 0)),
        out_shape=jax.ShapeDtypeStruct((bsz, n_tok, n_p * LANES), BF16),
        compiler_params=_cparams(("arbitrary", "arbitrary")),
        name="na_attention",
    )(qa, ka, ka, ka, va, va, va, ka_c, va_c, bias_tab)


def _diff_kernel(lam_ref, q_ref, k_ref, kc_ref, v_ref, vc_ref, g_ref, o_ref, *, out_scale, with_latent):
    lam = lam_ref[0, 0]
    for h in range(DIFF_HEADS):
        if with_latent:
            ks, vs = [k_ref[0, h], kc_ref[0, h]], [v_ref[0, h], vc_ref[0, h]]
        else:
            ks, vs = [kc_ref[0, h]], [vc_ref[0, h]]
        q0, q1 = _head_halves(q_ref[0, h])
        y = _diff_attend(q0, q1, ks, ks, vs, lam, g_ref[...], out_scale)
        o_ref[0, :, h * DIFF_VDIM:(h + 1) * DIFF_VDIM] = y.astype(BF16)


def _diff_attention(lam, qb, kb, vb, kb_c, vb_c, subln_g, *, tq, out_scale):
    bsz, _, n_tok, _ = qb.shape
    n_ctx = kb_c.shape[2]
    full = lambda a: pl.BlockSpec((1,) + a.shape[1:], lambda b, i: (b, 0, 0, 0))
    return pl.pallas_call(
        functools.partial(_diff_kernel, out_scale=out_scale, with_latent=True),
        grid=(bsz, n_tok // tq),
        in_specs=[pl.BlockSpec(memory_space=pltpu.SMEM),
                  pl.BlockSpec((1, DIFF_HEADS, tq, LANES), lambda b, i: (b, 0, i, 0)),
                  full(kb), full(kb_c), full(vb), full(vb_c),
                  pl.BlockSpec((1, DIFF_VDIM), lambda b, i: (0, 0))],
        out_specs=pl.BlockSpec((1, tq, DIFF_HEADS * DIFF_VDIM), lambda b, i: (b, i, 0)),
        out_shape=jax.ShapeDtypeStruct((bsz, n_tok, DIFF_HEADS * DIFF_VDIM), BF16),
        compiler_params=_cparams(("arbitrary", "arbitrary")),
        name="diff_attention",
    )(lam, qb, kb, kb_c, vb, vb_c, subln_g.reshape(1, DIFF_VDIM))


def _ctx_attn0_kernel(lam_ref, qa_ref, ka_ref, va_ref, qb_ref, kb_ref, vb_ref, g_ref, ya_ref, yb_ref,
                      *, out_scale):
    for p in range(NA_HEADS // 2):
        outs = [_softmax_attend(q, [ka_ref[0, p]], [va_ref[0, p]], [None]) for q in _head_halves(qa_ref[0, p])]
        ya_ref[0, :, p * LANES:(p + 1) * LANES] = _merge_halves(*outs).astype(BF16)
    _diff_kernel(lam_ref, qb_ref, None, kb_ref, None, vb_ref, g_ref, yb_ref,
                 out_scale=out_scale, with_latent=False)


def _ctx_attention0(lam, qa, ka, va, qb, kb, vb, subln_g, *, out_scale):
    bsz, _, n_ctx, _ = qa.shape
    full = lambda a: pl.BlockSpec((1,) + a.shape[1:], lambda b: (b, 0, 0, 0))
    width = NA_HEADS * HEAD_DIM
    out = jax.ShapeDtypeStruct((bsz, n_ctx, width), BF16)
    return pl.pallas_call(
        functools.partial(_ctx_attn0_kernel, out_scale=out_scale),
        grid=(bsz,),
        in_specs=[pl.BlockSpec(memory_space=pltpu.SMEM),
                  full(qa), full(ka), full(va), full(qb), full(kb), full(vb),
                  pl.BlockSpec((1, DIFF_VDIM), lambda b: (0, 0))],
        out_specs=[pl.BlockSpec((1, n_ctx, width), lambda b: (b, 0, 0))] * 2,
        out_shape=[out, out],
        compiler_params=_cparams(("arbitrary",)),
        name="ctx_attention0",
    )(lam, qa, ka, va, qb, kb, vb, subln_g.reshape(1, DIFF_VDIM))


FFN_CHUNK = FFN_DIM // 2


def _post0_kernel(x_ref, ya_ref, yb_ref, g1_ref, sh2_ref, sc2_ref, g2_ref, wo_ref, w13_ref, w2_ref,
                  ln1g_ref, ln1b_ref, ln2g_ref, ln2b_ref, o_ref):
    half = wo_ref.shape[0] // 2
    y = _dot(ya_ref[0], wo_ref[:half, :]) + _dot(yb_ref[0], wo_ref[half:, :])
    x1 = _layer_norm(DEEPNORM_ALPHA * x_ref[0] + g1_ref[0] * y, ln1g_ref[...], ln1b_ref[...])
    h = (x1 * (1.0 + sc2_ref[0]) + sh2_ref[0]).astype(BF16)
    f = None
    for c in range(FFN_DIM // FFN_CHUNK):
        a = _dot(h, w13_ref[:, c * FFN_CHUNK:(c + 1) * FFN_CHUNK])
        g = _dot(h, w13_ref[:, FFN_DIM + c * FFN_CHUNK:FFN_DIM + (c + 1) * FFN_CHUNK])
        t = _dot((_silu(a) * g).astype(BF16), w2_ref[c * FFN_CHUNK:(c + 1) * FFN_CHUNK, :])
        f = t if f is None else f + t
    o_ref[0] = _layer_norm(DEEPNORM_ALPHA * x1 + g2_ref[0] * f, ln2g_ref[...], ln2b_ref[...])


def _post0(x, ya, yb, g1, sh2, sc2, g2, wo, w13, w2, ln1g, ln1b, ln2g, ln2b, *, tm):
    bsz, n_tok, _ = x.shape
    per_batch = g1.shape[0] > 1
    mod_map = (lambda b, i: (b, 0, 0)) if per_batch else (lambda b, i: (0, 0, 0))
    mod = pl.BlockSpec((1, 1, D_MODEL), mod_map)
    const = lambda a: pl.BlockSpec(a.shape, lambda b, i: (0,) * a.ndim, pipeline_mode=pl.Buffered(1))
    vec = lambda a: a.reshape(1, D_MODEL)
    half = ya.shape[2]
    return pl.pallas_call(
        _post0_kernel,
        grid=(bsz, n_tok // tm),
        in_specs=[pl.BlockSpec((1, tm, D_MODEL), lambda b, i: (b, i, 0)),
                  pl.BlockSpec((1, tm, half), lambda b, i: (b, i, 0)),
                  pl.BlockSpec((1, tm, half), lambda b, i: (b, i, 0)),
                  mod, mod, mod, mod, const(wo), const(w13), const(w2),
                  const(vec(ln1g)), const(vec(ln1b)), const(vec(ln2g)), const(vec(ln2b))],
        out_specs=pl.BlockSpec((1, tm, D_MODEL), lambda b, i: (b, i, 0)),
        out_shape=jax.ShapeDtypeStruct(x.shape, F32),
        compiler_params=_cparams(("arbitrary", "arbitrary")),
        name="post0",
    )(x, ya, yb, g1, sh2, sc2, g2, wo, w13, w2, vec(ln1g), vec(ln1b), vec(ln2g), vec(ln2b))


NORM_CHUNK = 4 * HEAD_DIM


def _inproj1_kernel(x_ref, sh_ref, sc_ref, w_ref, gain_ref, ind_ref, indt_ref, cos_ref, sin_ref, *out_refs,
                    n_norm_chunks, out_layout, rope):
    h = (x_ref[0] * (1.0 + sc_ref[0]) + sh_ref[0]).astype(BF16)
    lane_hi = (lax.broadcasted_iota(jnp.int32, (1, LANES), 1) % HEAD_DIM) >= HEAD_DIM // 2
    targets = iter([(ref, i, dup) for ref, (n, dup) in zip(out_refs, out_layout)
                    for i in range(0, n, 2 if dup else 1)])
    n_chunks = w_ref.shape[1] // NORM_CHUNK
    for c in range(n_chunks):
        cols = slice(c * NORM_CHUNK, (c + 1) * NORM_CHUNK)
        p = _dot(h, w_ref[:, cols])
        if c < n_norm_chunks:
            ms = _dot2(p * p, ind_ref[...]) * (1.0 / HEAD_DIM)
            r = _dot2(lax.rsqrt(ms + RMS_EPS), indt_ref[...])
            p = p * r * gain_ref[:, cols]
            if rope:
                p = jnp.concatenate(
                    [_rope128(p[:, j * LANES:(j + 1) * LANES], cos_ref[...], sin_ref[...], lane_hi)
                     for j in range(NORM_CHUNK // LANES)], axis=1)
        _store_pairs(p, targets)


def _inproj1(x, shift, scale, w, gain, cos, sin, *, tm, out_layout, n_norm_chunks, rope):
    bsz, n_tok, _ = x.shape
    per_batch = shift.shape[0] > 1
    mod_map = (lambda b, i: (b, 0, 0)) if per_batch else (lambda b, i: (0, 0, 0))
    ind_np = np.zeros((NORM_CHUNK, LANES), np.float32)
    ind_np[np.arange(NORM_CHUNK), np.arange(NORM_CHUNK) // HEAD_DIM] = 1.0
    ind = jnp.asarray(ind_np, BF16)
    indt = jnp.asarray(ind_np.T, BF16)
    const = lambda a: pl.BlockSpec(a.shape, lambda b, i: (0,) * a.ndim)
    return pl.pallas_call(
        functools.partial(_inproj1_kernel, n_norm_chunks=n_norm_chunks, out_layout=out_layout, rope=rope),
        grid=(bsz, n_tok // tm),
        in_specs=[pl.BlockSpec((1, tm, D_MODEL), lambda b, i: (b, i, 0)),
                  pl.BlockSpec((1, 1, D_MODEL), mod_map),
                  pl.BlockSpec((1, 1, D_MODEL), mod_map),
                  const(w), const(gain), const(ind), const(indt),
                  pl.BlockSpec((tm, LANES), lambda b, i: (i, 0)),
                  pl.BlockSpec((tm, LANES), lambda b, i: (i, 0))],
        out_specs=[pl.BlockSpec((1, n, tm, LANES), lambda b, i: (b, 0, i, 0)) for n, _ in out_layout],
        out_shape=[jax.ShapeDtypeStruct((bsz, n, n_tok, LANES), BF16) for n, _ in out_layout],
        compiler_params=_cparams(("arbitrary", "arbitrary")),
        name="inproj1",
    )(x, shift, scale, w, gain, ind, indt, cos, sin)


def _gqa_kernel(q_ref, k_ref, kc_ref, v_ref, vc_ref, o_ref):
    for p in range(GQA_Q_HEADS // 2):
        g = 2 * p // GQA_GROUP
        ks, vs = [k_ref[0, g], kc_ref[0, g]], [v_ref[0, g], vc_ref[0, g]]
        outs = [_softmax_attend(q, ks, vs, [None, None]) for q in _head_halves(q_ref[0, p])]
        o_ref[0, :, p * LANES:(p + 1) * LANES] = _merge_halves(*outs).astype(BF16)


def _gqa_attention(q, k, v, k_c, v_c, *, tq):
    bsz, n_p, n_tok, _ = q.shape
    full = lambda a: pl.BlockSpec((1,) + a.shape[1:], lambda b, i: (b, 0, 0, 0))
    return pl.pallas_call(
        _gqa_kernel,
        grid=(bsz, n_tok // tq),
        in_specs=[pl.BlockSpec((1, n_p, tq, LANES), lambda b, i: (b, 0, i, 0)),
                  full(k), full(k_c), full(v), full(v_c)],
        out_specs=pl.BlockSpec((1, tq, n_p * LANES), lambda b, i: (b, i, 0)),
        out_shape=jax.ShapeDtypeStruct((bsz, n_tok, n_p * LANES), BF16),
        compiler_params=_cparams(("arbitrary", "arbitrary")),
        name="gqa_attention",
    )(q, k, k_c, v, v_c)


def _out1_kernel(x_ref, y_ref, g1_ref, sh2_ref, sc2_ref, wo_ref, lng_ref, lnb_ref, rw_ref,
                 x1_ref, hm_ref, route_ref):
    rw_hi, rw_lo = _split(rw_ref[...])
    n_sub = 2
    rows_sub = x_ref.shape[1] // n_sub
    for sub in range(n_sub):
        rows = slice(sub * rows_sub, (sub + 1) * rows_sub)
        y = _dot(y_ref[0, rows, :], wo_ref[...])
        x1 = _layer_norm(DEEPNORM_ALPHA * x_ref[0, rows, :] + g1_ref[0] * y, lng_ref[...], lnb_ref[...])
        x1_ref[0, rows, :] = x1
        hm = x1 * (1.0 + sc2_ref[0]) + sh2_ref[0]
        _to_row_tiles(hm, hm_ref.at[0, pl.ds(sub * rows_sub * ROW_TILE, rows_sub * ROW_TILE)])
        lane = lax.broadcasted_iota(jnp.int32, (rows_sub, LANES), 1)
        hm_hi, hm_lo = _split(hm)
        logits = _dot(hm_hi, rw_hi) + (_dot(hm_hi, rw_lo) + _dot(hm_lo, rw_hi))
        logits = jnp.where(lane < N_EXPERTS, logits, NEG_INF)
        v1 = logits.max(axis=-1, keepdims=True)
        i1 = jnp.where(logits == v1, lane, LANES).min(axis=-1, keepdims=True)
        rest = jnp.where(lane == i1, NEG_INF, logits)
        v2 = rest.max(axis=-1, keepdims=True)
        i2 = jnp.where(rest == v2, lane, LANES).min(axis=-1, keepdims=True)
        e = jnp.exp(v2 - v1)
        inv = 1.0 / (1.0 + e)
        route_ref[0, rows, :] = jnp.where(
            lane == 0, i1.astype(F32),
            jnp.where(lane == 1, i2.astype(F32), jnp.where(lane == 2, inv, jnp.where(lane == 3, e * inv, 0.0))))


def _out1(x, y, g1, sh2, sc2, wo, lng, lnb, rw, *, tm):
    bsz, n_tok, _ = x.shape
    mod = pl.BlockSpec((1, 1, D_MODEL), lambda b, i: (b, 0, 0))
    tile = pl.BlockSpec((1, tm, D_MODEL), lambda b, i: (b, i, 0))
    const = lambda a: pl.BlockSpec(a.shape, lambda b, i: (0,) * a.ndim)
    vec = lambda a: a.reshape(1, D_MODEL)
    return pl.pallas_call(
        _out1_kernel,
        grid=(bsz, n_tok // tm),
        in_specs=[tile, tile, mod, mod, mod, const(wo), const(vec(lng)), const(vec(lnb)), const(rw)],
        out_specs=[tile, pl.BlockSpec((1, tm * ROW_TILE, LANES), lambda b, i: (b, i, 0)),
                   pl.BlockSpec((1, tm, LANES), lambda b, i: (b, i, 0))],
        out_shape=[jax.ShapeDtypeStruct(x.shape, F32), jax.ShapeDtypeStruct((bsz, n_tok * ROW_TILE, LANES), F32),
                   jax.ShapeDtypeStruct((bsz, n_tok, LANES), F32)],
        compiler_params=_cparams(("arbitrary", "arbitrary")),
        name="out1",
    )(x, y, g1, sh2, sc2, wo, vec(lng), vec(lnb), rw)


def _moe_kernel(be_ref, nv_ref, gfirst_ref, gnext_ref, sprev_ref, slast_ref, hm_ref, w1_ref, w3_ref, w2_ref,
                y_ref, xg_ref, xb_ref, hid_ref, acc_ref, out_ref, gsem, ssem):
    del be_ref
    i, j = pl.program_id(0), pl.program_id(1)
    n_valid = nv_ref[0]
    slot = i % 2
    other = 1 - slot
    tile_rows = lambda row: pl.ds(pl.multiple_of(row * ROW_TILE, ROW_TILE), ROW_TILE)

    def gather_row(idx_ref, row, dst_slot):
        return pltpu.make_async_copy(hm_ref.at[tile_rows(idx_ref[0, 0, row])],
                                     xg_ref.at[dst_slot, tile_rows(row)], gsem.at[dst_slot])

    def scatter_row(idx_ref, row, src_slot):
        return pltpu.make_async_copy(out_ref.at[src_slot, tile_rows(row)],
                                     y_ref.at[tile_rows(idx_ref[0, 0, row])], ssem.at[src_slot])

    def wait_gather(s):
        pltpu.make_async_copy(hm_ref.at[pl.ds(0, MOE_TM * ROW_TILE)], xg_ref.at[s], gsem.at[s]).wait()

    def wait_scatter(s):
        pltpu.make_async_copy(out_ref.at[s], y_ref.at[pl.ds(0, MOE_TM * ROW_TILE)], ssem.at[s]).wait()

    @pl.when(jnp.logical_and(i == 0, j == 0))
    def _():
        out_ref[1] = jnp.zeros(out_ref.shape[1:], F32)

        def start(r, carry):
            gather_row(gfirst_ref, r, 0).start()
            return carry

        lax.fori_loop(0, MOE_TM, start, 0)
        n_real = y_ref.shape[0] - 2 * MOE_TM * ROW_TILE
        fill = pltpu.make_async_copy(out_ref.at[1], y_ref.at[pl.ds(n_real, MOE_TM * ROW_TILE)], ssem.at[1])
        fill.start()
        fill.wait()

    @pl.when(jnp.logical_and(i < n_valid, j == 0))
    def _():
        wait_gather(slot)
        xb_ref[...] = _from_row_tiles(xg_ref.at[slot], MOE_TM).astype(BF16)

        @pl.when(i >= 1)
        def _():
            wait_scatter(slot)

        acc_ref[...] = jnp.zeros(acc_ref.shape, F32)

    @pl.when(i < n_valid)
    def _():
        base = j * MOE_ROWS_PER_STEP
        n_up = MOE_TF // MOE_CHUNK
        n_down = D_MODEL // MOE_CHUNK
        per_group = -(-MOE_ROWS_PER_STEP // (n_up + n_down))
        rows = iter(range(MOE_ROWS_PER_STEP))

        def copy_group():
            for _ in range(per_group):
                r = next(rows, None)
                if r is not None:
                    gather_row(gnext_ref, base + r, other).start(priority=r % 2)
                    scatter_row(sprev_ref, base + r, other).start(priority=r % 2)

        cols = lambda n: slice(n * MOE_CHUNK, (n + 1) * MOE_CHUNK)
        up = lambda n: (_dot(xb_ref[...], w1_ref[0, :, cols(n)]), _dot(xb_ref[...], w3_ref[0, :, cols(n)]))
        pending = up(0)
        for n in range(n_up):
            following = up(n + 1) if n + 1 < n_up else None
            a, g = pending
            hid_ref[:, cols(n)] = (_silu(a) * g).astype(BF16)
            copy_group()
            pending = following
        down = lambda n: _dot(hid_ref[...], w2_ref[0, :, cols(n)])
        pending = down(0)
        for n in range(n_down):
            following = down(n + 1) if n + 1 < n_down else None
            acc_ref[:, cols(n)] += pending
            copy_group()
            pending = following

        @pl.when(j == MOE_NF - 1)
        def _():
            _to_row_tiles(acc_ref[...], out_ref.at[slot])

    @pl.when(jnp.logical_and(i == n_valid - 1, j == MOE_NF - 1))
    def _():
        def start(r, carry):
            scatter_row(slast_ref, r, slot).start()
            return carry

        lax.fori_loop(0, MOE_TM, start, 0)
        wait_gather(other)
        wait_scatter(other)
        wait_scatter(slot)


def _moe_experts(hm_packed, plan, w13, w2):
    block_e, n_valid, gidx, sidx, n_out_rows = plan
    n_blocks = block_e.shape[0]
    smem_rows = lambda index_map: pl.BlockSpec((1, 1, MOE_TM), index_map, memory_space=pltpu.SMEM)
    return pl.pallas_call(
        _moe_kernel,
        grid_spec=pltpu.PrefetchScalarGridSpec(
            num_scalar_prefetch=2,
            grid=(n_blocks, MOE_NF),
            in_specs=[smem_rows(lambda i, j, be, nv: (0, 0, 0)),
                      smem_rows(lambda i, j, be, nv: (i + 1, 0, 0)),
                      smem_rows(lambda i, j, be, nv: (i, 0, 0)),
                      smem_rows(lambda i, j, be, nv: (nv[0], 0, 0)),
                      pl.BlockSpec(memory_space=pl.ANY),
                      pl.BlockSpec((1, D_MODEL, MOE_TF), lambda i, j, be, nv: (be[i], 0, j)),
                      pl.BlockSpec((1, D_MODEL, MOE_TF), lambda i, j, be, nv: (be[i], 0, MOE_NF + j)),
                      pl.BlockSpec((1, MOE_TF, D_MODEL), lambda i, j, be, nv: (be[i], j, 0))],
            out_specs=pl.BlockSpec(memory_space=pl.ANY),
            scratch_shapes=[pltpu.VMEM((2, MOE_TM * ROW_TILE, LANES), F32),
                            pltpu.VMEM((MOE_TM, D_MODEL), BF16),
                            pltpu.VMEM((MOE_TM, MOE_TF), BF16),
                            pltpu.VMEM((MOE_TM, D_MODEL), F32),
                            pltpu.VMEM((2, MOE_TM * ROW_TILE, LANES), F32),
                            pltpu.SemaphoreType.DMA((2,)),
                            pltpu.SemaphoreType.DMA((2,))]),
        out_shape=jax.ShapeDtypeStruct((n_out_rows * ROW_TILE, LANES), F32),
        compiler_params=_cparams(("arbitrary", "arbitrary")),
        name="moe_experts",
    )(block_e, n_valid, gidx, gidx, sidx, sidx, hm_packed, w13, w13, w2)


def _combine_kernel(x1_ref, y0_ref, y1_ref, route_ref, g2_ref, lng_ref, lnb_ref, o_ref):
    route = route_ref[...]
    n = x1_ref.shape[0]
    m = _from_row_tiles(y0_ref, n) * route[:, 2:3] + _from_row_tiles(y1_ref, n) * route[:, 3:4]
    o_ref[...] = _layer_norm(DEEPNORM_ALPHA * x1_ref[...] + g2_ref[0] * m, lng_ref[...], lnb_ref[...])


def _combine(x1, y_rows, route, g2, lng, lnb, *, tm, tiles_per_batch):
    n_tok = x1.shape[0]
    y_tile = lambda first: pl.BlockSpec((tm * ROW_TILE, LANES), lambda i: (i + first // tm, 0))
    vec = lambda a: a.reshape(1, D_MODEL)
    const = lambda a: pl.BlockSpec(a.shape, lambda i: (0,) * a.ndim)
    return pl.pallas_call(
        _combine_kernel,
        grid=(n_tok // tm,),
        in_specs=[pl.BlockSpec((tm, D_MODEL), lambda i: (i, 0)),
                  y_tile(0), y_tile(n_tok),
                  pl.BlockSpec((tm, LANES), lambda i: (i, 0)),
                  pl.BlockSpec((1, 1, D_MODEL), lambda i: (i // tiles_per_batch, 0, 0)),
                  const(vec(lng)), const(vec(lnb))],
        out_specs=pl.BlockSpec((tm, D_MODEL), lambda i: (i, 0)),
        out_shape=jax.ShapeDtypeStruct(x1.shape, F32),
        compiler_params=_cparams(("arbitrary",)),
        name="moe_combine",
    )(x1, y_rows, y_rows, route, g2, vec(lng), vec(lnb))


def _routing_plan(route, n_tok):
    n_pairs = n_tok * TOP_K
    flat_e = route[:, :TOP_K].astype(jnp.int32).reshape(-1)
    pair = jnp.arange(n_pairs, dtype=jnp.int32)
    sorted_key = jnp.sort(flat_e * n_pairs + pair)
    order = sorted_key - (sorted_key // n_pairs) * n_pairs
    experts = jnp.arange(N_EXPERTS, dtype=jnp.int32)
    counts = jnp.sum((flat_e[:, None] == experts[None, :]).astype(jnp.int32), axis=0)
    first_pair = jnp.cumsum(counts) - counts
    blocks_per = (counts + MOE_TM - 1) // MOE_TM
    block_end = jnp.cumsum(blocks_per)
    block_start = block_end - blocks_per
    n_blocks = -(-(n_pairs + N_EXPERTS * (MOE_TM - 1)) // MOE_TM)
    blk = jnp.arange(n_blocks, dtype=jnp.int32)
    block_e = jnp.minimum(jnp.sum((block_end[None, :] <= blk[:, None]).astype(jnp.int32), axis=1), N_EXPERTS - 1)
    n_valid = block_end[-1]
    sel = (block_e[:, None] == experts[None, :]).astype(jnp.int32)
    pick = lambda v: jnp.sum(sel * v[None, :], axis=1)
    offset = (blk - pick(block_start)) * MOE_TM
    src = jnp.clip(pick(first_pair) + offset, 0, n_pairs)
    r = jnp.arange(MOE_TM, dtype=jnp.int32)[None, :]
    pairs = jnp.take(order, jnp.minimum(src[:, None] + r, n_pairs - 1), axis=0)
    used = jnp.logical_and((blk < n_valid)[:, None], offset[:, None] + r < pick(counts)[:, None])
    spare = n_pairs + (blk % 2)[:, None] * MOE_TM + r
    token, choice = pairs // TOP_K, pairs % TOP_K
    gidx = jnp.where(used, token, 0)
    sidx = jnp.where(used, choice * n_tok + token, spare)
    gidx = jnp.concatenate([gidx, jnp.zeros((1, MOE_TM), jnp.int32)], axis=0)
    sidx = jnp.concatenate([n_pairs + MOE_TM + r, sidx], axis=0)
    shape3 = (n_blocks + 1, 1, MOE_TM)
    return (block_e.astype(jnp.int32), n_valid.astype(jnp.int32).reshape(1), gidx.reshape(shape3),
            sidx.reshape(shape3), n_pairs + 2 * MOE_TM)


def _rope_tables(n_tok):
    t = jnp.arange(n_tok, dtype=jnp.int32)
    n_freq = HEAD_DIM // 4
    inv_freq = ROPE_THETA ** (-jnp.arange(n_freq, dtype=F32) / n_freq)
    ang = jnp.concatenate([(t // GRID_W).astype(F32)[:, None] * inv_freq,
                           (t % GRID_W).astype(F32)[:, None] * inv_freq], axis=-1)
    cos, sin = jnp.cos(ang), jnp.sin(ang)
    cos128 = jnp.tile(cos, (1, 4))
    sin128 = jnp.tile(jnp.concatenate([-sin, sin], axis=-1), (1, 2))
    return cos128, sin128


def _mods(cond_rows, ada_w, ada_b, bsz):
    out = _ada_mods(cond_rows, ada_w, ada_b)
    lat = [m.reshape(bsz, 1, D_MODEL) for m in jnp.split(out[:bsz], 6, axis=-1)]
    ctx = [m.reshape(1, 1, D_MODEL) for m in jnp.split(out[bsz:bsz + 1], 6, axis=-1)]
    return lat, ctx


def kernel(x, c, ctx, c_ctx, l0_ada_w, l0_ada_b, l0_w_in, l0_rpb, l0_lambda_qk, l0_subln_g, l0_w_out, l0_ln1_g, l0_ln1_b, l0_ffn_w13, l0_ffn_w2, l0_ln2_g, l0_ln2_b, l1_ada_w, l1_ada_b, l1_w_in, l1_q_norm_g, l1_k_norm_g, l1_w_out, l1_ln1_g, l1_ln1_b, l1_router_w, l1_moe_w13, l1_moe_w2, l1_ln2_g, l1_ln2_b):
    bsz, n_tok, _ = x.shape
    n_ctx = ctx.shape[1]
    assert n_tok == GRID_ROWS * GRID_W and n_tok % TOKEN_TILE == 0 and n_tok % ATTN_TQ == 0
    assert n_ctx % NA_KEY_CHUNK == 0 and bsz < COND_ROWS
    cond_rows = jnp.concatenate([c, c_ctx[None, :], jnp.zeros((COND_ROWS - bsz - 1, D_MODEL), F32)], axis=0)
    cos, sin = _rope_tables(n_tok)
    ones = jnp.ones((n_ctx, LANES), F32)
    zeros = jnp.zeros((n_ctx, LANES), F32)

    (sh1, sc1, g1, sh2, sc2, g2), (csh1, csc1, cg1, csh2, csc2, cg2) = _mods(cond_rows, l0_ada_w, l0_ada_b, bsz)
    w_in0 = l0_w_in.astype(BF16)
    qa, ka, va, qb, kb, vb = _inproj0(x, sh1, sc1, w_in0, cos, sin, tm=TOKEN_TILE, rope=True)
    qa_c, ka_c, va_c, qb_c, kb_c, vb_c = _inproj0(ctx, csh1, csc1, w_in0, ones, zeros, tm=n_ctx, rope=False)
    lam_init = 0.8 - 0.6 * math.exp(-0.3 * 0)
    lq = l0_lambda_qk.astype(F32)
    lam = (jnp.exp(jnp.sum(lq[0] * lq[1])) - jnp.exp(jnp.sum(lq[2] * lq[3])) + lam_init).reshape(1, 1)
    y_a = _na_attention(qa, ka, va, ka_c, va_c, _na_bias_table(l0_rpb))
    y_b = _diff_attention(lam, qb, kb, vb, kb_c, vb_c, l0_subln_g, tq=ATTN_TQ, out_scale=1.0 - lam_init)
    ya_c, yb_c = _ctx_attention0(lam, qa_c, ka_c, va_c, qb_c, kb_c, vb_c, l0_subln_g, out_scale=1.0 - lam_init)
    wo0, w13_0, w2_0 = l0_w_out.astype(BF16), l0_ffn_w13.astype(BF16), l0_ffn_w2.astype(BF16)
    ln0 = (l0_ln1_g, l0_ln1_b, l0_ln2_g, l0_ln2_b)
    x = _post0(x, y_a, y_b, g1, sh2, sc2, g2, wo0, w13_0, w2_0, *ln0, tm=TOKEN_TILE)
    ctx = _post0(ctx, ya_c, yb_c, cg1, csh2, csc2, cg2, wo0, w13_0, w2_0, *ln0, tm=n_ctx)

    (sh1, sc1, g1, sh2, sc2, g2), (csh1, csc1, _, _, _, _) = _mods(cond_rows, l1_ada_w, l1_ada_b, bsz)
    w_in1 = l1_w_in.astype(BF16)
    n_q, n_kv = GQA_Q_HEADS * HEAD_DIM, GQA_KV_HEADS * HEAD_DIM
    gain = jnp.concatenate([jnp.tile(l1_q_norm_g, GQA_Q_HEADS) * QK_SCALE,
                            jnp.tile(l1_k_norm_g, GQA_KV_HEADS)]).reshape(1, n_q + n_kv)
    q, k, v = _inproj1(x, sh1, sc1, w_in1, gain, cos, sin, tm=TOKEN_TILE,
                       out_layout=((GQA_Q_HEADS // 2, False), (GQA_KV_HEADS, True), (GQA_KV_HEADS, True)),
                       n_norm_chunks=(n_q + n_kv) // NORM_CHUNK, rope=True)
    k_c, v_c = _inproj1(ctx, csh1, csc1, w_in1[:, n_q:], gain[:, n_q:], ones, zeros, tm=n_ctx,
                        out_layout=((GQA_KV_HEADS, True), (GQA_KV_HEADS, True)),
                        n_norm_chunks=n_kv // NORM_CHUNK, rope=False)
    y = _gqa_attention(q, k, v, k_c, v_c, tq=ATTN_TQ)
    rw = jnp.pad(l1_router_w.astype(F32), ((0, 0), (0, LANES - N_EXPERTS)))
    x1, hm, route = _out1(x, y, g1, sh2, sc2, l1_w_out.astype(BF16), l1_ln1_g, l1_ln1_b, rw, tm=TOKEN_TILE)
    n_all = bsz * n_tok
    x1, hm, route = x1.reshape(n_all, D_MODEL), hm.reshape(n_all * ROW_TILE, LANES), route.reshape(n_all, LANES)
    y_rows = _moe_experts(hm, _routing_plan(route, n_all), l1_moe_w13.astype(BF16), l1_moe_w2.astype(BF16))
    out = _combine(x1, y_rows, route, g2, l1_ln2_g, l1_ln2_b, tm=TOKEN_TILE, tiles_per_batch=n_tok // TOKEN_TILE)
    return out.reshape(bsz, n_tok, D_MODEL)
```
